```python
import math
import jax, jax.numpy as jnp
from jax import lax
import numpy as np

D_MODEL = 1024
BATCH = 4
SEQ = 8192
DEPTH = 1
DEC_BATCH = 8
DEC_SEQ = 4096
PAST_LEN = 128

GROUP_SIZE = 64
CONV_WIDTH = D_MODEL // 2
HYENA_WIDTH = D_MODEL - CONV_WIDTH
CONV_GROUPS = CONV_WIDTH // GROUP_SIZE
HYENA_GROUPS = HYENA_WIDTH // GROUP_SIZE
PROJ_WIDTH = 3 * CONV_WIDTH + 3 * HYENA_WIDTH
SHORT_K = 3
POS_BANDS = 16
POS_EMB = 1 + 2 * POS_BANDS
FILTER_HIDDEN = 64
DECAY_TARGET = 1e-2
FAST_DECAY_PCT = 0.3
SLOW_DECAY_PCT = 1.5
PEER_HEADS = 8
PEER_KEYS = 128
PEER_EXPERTS = PEER_KEYS * PEER_KEYS
PEER_TOPK = 16
PEER_QDIM = 256
PEER_HALF = PEER_QDIM // 2
PEER_TOKEN_BLOCK = 128
ALPHA = (2.0 * DEPTH) ** 0.25
BETA = (8.0 * DEPTH) ** -0.25
LN_EPS = 1e-5

kernel_name = 'hybrid_conv_hyena_peer_encoder'

F32 = jnp.float32


def layer_norm(x, g, b):
    xf = x.astype(F32)
    mu = jnp.mean(xf, -1, keepdims=True)
    var = jnp.mean(jnp.square(xf - mu), -1, keepdims=True)
    return ((xf - mu) * lax.rsqrt(var + LN_EPS) * g.astype(F32) + b.astype(F32)).astype(x.dtype)


def short_conv(z, w):
    L = z.shape[1]
    pad = SHORT_K // 2
    zp = jnp.pad(z, ((0, 0), (pad, pad), (0, 0)))
    out = zp[:, 0:L] * w[0]
    for j in range(1, SHORT_K):
        out = out + zp[:, j:j + L] * w[j]
    return out


def hyena_filter(L, w1, b1, freq, w2, b2, w3, decay):
    t = jnp.linspace(0.0, 1.0, L, dtype=F32)[:, None]
    w = (2.0 * math.pi) * jnp.arange(L, dtype=F32)[:, None] / L
    f = jnp.linspace(1e-4, POS_BANDS - 1, POS_BANDS, dtype=F32)[None, :]
    feat = jnp.concatenate([t, jnp.cos(f * w), -jnp.sin(f * w)], -1)
    fr = freq.astype(F32)
    h = jnp.sin(fr * (feat @ w1.astype(F32) + b1.astype(F32)))
    h = jnp.sin(fr * (h @ w2.astype(F32) + b2.astype(F32)))
    h = (h @ w3.astype(F32)).reshape(L, 2, HYENA_WIDTH)
    h = h * jnp.exp(-t[:, :, None] * jnp.abs(decay.astype(F32))[None])
    fwd, bwd = h[:, 0], h[:, 1]
    k = jnp.concatenate([fwd, jnp.zeros((1, HYENA_WIDTH), F32), bwd[:0:-1]], 0)
    return k / jnp.sum(jnp.abs(k), 0, keepdims=True)


def long_conv(z, k):
    L = z.shape[1]
    n = 2 * L
    zf = jnp.fft.rfft(z.astype(F32), n=n, axis=1)
    kf = jnp.fft.rfft(k, n=n, axis=0)
    return jnp.fft.irfft(zf * kf[None], n=n, axis=1)[:, :L].astype(z.dtype)


def peer(x, wq, keys, u, v):
    B, L, D = x.shape
    xs = x.reshape(-1, PEER_TOKEN_BLOCK, D)

    def block(xb):
        T = xb.shape[0]
        q = (xb @ wq).reshape(T, PEER_HEADS, 2, PEER_HALF)
        s = jnp.einsum('thpk,hpnk->thpn', q, keys).astype(F32)
        s_top, i_top = lax.top_k(s, PEER_TOPK)
        cand = (s_top[:, :, 0, :, None] + s_top[:, :, 1, None, :]).reshape(T, PEER_HEADS, PEER_TOPK * PEER_TOPK)
        cidx = (i_top[:, :, 0, :, None] * PEER_KEYS + i_top[:, :, 1, None, :]).reshape(T, PEER_HEADS, PEER_TOPK * PEER_TOPK)
        best, pos = lax.top_k(cand, PEER_TOPK)
        eidx = jnp.take_along_axis(cidx, pos, -1)
        g = jax.nn.softmax(best, -1)
        u_sel = jnp.take(u, eidx, axis=0)
        act = jax.nn.gelu(jnp.einsum('thkd,td->thk', u_sel, xb).astype(F32), approximate=False)
        coef = (g * act).astype(x.dtype)
        v_sel = jnp.take(v, eidx, axis=0)
        return jnp.einsum('thk,thkd->td', coef, v_sel)

    return lax.map(block, xs).reshape(B, L, D)


def encoder_layer(x, w_in, b_in, a_conv_w, h_conv_w, h_conv_b, hf_w1, hf_b1, hf_freq, hf_w2, hf_b2,
                  hf_w3, hy_decay, hy_bias, w_out, ln1_g, ln1_b, peer_wq, peer_keys, peer_u, peer_v,
                  ln2_g, ln2_b):
    L = x.shape[1]
    p = x @ w_in + b_in
    cw = CONV_WIDTH
    a_b, a_c, a_h = p[..., :cw], p[..., cw:2 * cw], p[..., 2 * cw:3 * cw]
    y_a = a_b * short_conv(a_c * a_h, a_conv_w)
    hy = short_conv(p[..., 3 * cw:], h_conv_w) + h_conv_b
    hw = HYENA_WIDTH
    x0, x1, hv = hy[..., :hw], hy[..., hw:2 * hw], hy[..., 2 * hw:]
    k = hyena_filter(L, hf_w1, hf_b1, hf_freq, hf_w2, hf_b2, hf_w3, hy_decay)
    z = hv * x1
    z = long_conv(z, k) + z * hy_bias
    y_h = x0 * z
    mix = jnp.concatenate([y_a, y_h], -1) @ w_out
    x = layer_norm(ALPHA * x + mix, ln1_g, ln1_b)
    x = layer_norm(ALPHA * x + peer(x, peer_wq, peer_keys, peer_u, peer_v), ln2_g, ln2_b)
    return x


def setup_inputs(seed: int = 0) -> dict:
    key = jax.random.key(seed)
    ks = jax.random.split(key, 24)
    nrm = lambda k, shape, s: jax.random.normal(k, shape, F32) * s
    decay_lo = abs(math.log(DECAY_TARGET)) / SLOW_DECAY_PCT
    decay_hi = abs(math.log(DECAY_TARGET)) / FAST_DECAY_PCT
    decay_base = jnp.linspace(decay_lo, decay_hi, HYENA_WIDTH, dtype=F32)
    return {
        'x_prompt': nrm(ks[0], (BATCH, SEQ, D_MODEL), 1.0),
        'x_sample': nrm(ks[1], (DEC_BATCH, DEC_SEQ, D_MODEL), 1.0),
        'w_in': nrm(ks[2], (DEPTH, D_MODEL, PROJ_WIDTH), D_MODEL ** -0.5),
        'b_in': nrm(ks[3], (DEPTH, PROJ_WIDTH), 0.02),
        'a_conv_w': nrm(ks[4], (DEPTH, SHORT_K, CONV_WIDTH), SHORT_K ** -0.5),
        'h_conv_w': nrm(ks[5], (DEPTH, SHORT_K, 3 * HYENA_WIDTH), SHORT_K ** -0.5),
        'h_conv_b': nrm(ks[6], (DEPTH, 3 * HYENA_WIDTH), 0.02),
        'hf_w1': nrm(ks[7], (DEPTH, POS_EMB, FILTER_HIDDEN), POS_EMB ** -0.5),
        'hf_b1': nrm(ks[8], (DEPTH, FILTER_HIDDEN), 0.02),
        'hf_freq': 1.0 + nrm(ks[9], (DEPTH, FILTER_HIDDEN), 0.01),
        'hf_w2': nrm(ks[10], (DEPTH, FILTER_HIDDEN, FILTER_HIDDEN), FILTER_HIDDEN ** -0.5),
        'hf_b2': nrm(ks[11], (DEPTH, FILTER_HIDDEN), 0.02),
        'hf_w3': nrm(ks[12], (DEPTH, FILTER_HIDDEN, 2 * HYENA_WIDTH), FILTER_HIDDEN ** -0.5),
        'hy_decay': decay_base + nrm(ks[13], (DEPTH, 2, HYENA_WIDTH), 0.1),
        'hy_bias': nrm(ks[14], (DEPTH, HYENA_WIDTH), 1.0),
        'w_out': nrm(ks[15], (DEPTH, D_MODEL, D_MODEL), BETA * D_MODEL ** -0.5),
        'ln1_g': 1.0 + nrm(ks[16], (DEPTH, D_MODEL), 0.02),
        'ln1_b': nrm(ks[17], (DEPTH, D_MODEL), 0.02),
        'peer_wq': nrm(ks[18], (DEPTH, D_MODEL, PEER_HEADS * PEER_QDIM), D_MODEL ** -0.5),
        'peer_keys': nrm(ks[19], (DEPTH, PEER_HEADS, 2, PEER_KEYS, PEER_HALF), PEER_HALF ** -0.5),
        'peer_u': nrm(ks[20], (DEPTH, PEER_EXPERTS, D_MODEL), D_MODEL ** -0.5),
        'peer_v': nrm(ks[21], (DEPTH, PEER_EXPERTS, D_MODEL), BETA * PEER_HEADS ** -0.5),
        'ln2_g': 1.0 + nrm(ks[22], (DEPTH, D_MODEL), 0.02),
        'ln2_b': nrm(ks[23], (DEPTH, D_MODEL), 0.02),
    }


def reference(x_prompt, x_sample, w_in, b_in, a_conv_w, h_conv_w, h_conv_b, hf_w1, hf_b1, hf_freq,
              hf_w2, hf_b2, hf_w3, hy_decay, hy_bias, w_out, ln1_g, ln1_b, peer_wq, peer_keys,
              peer_u, peer_v, ln2_g, ln2_b):
    def trunk(x):
        for l in range(DEPTH):
            x = encoder_layer(x, w_in[l], b_in[l], a_conv_w[l], h_conv_w[l], h_conv_b[l], hf_w1[l],
                              hf_b1[l], hf_freq[l], hf_w2[l], hf_b2[l], hf_w3[l], hy_decay[l],
                              hy_bias[l], w_out[l], ln1_g[l], ln1_b[l], peer_wq[l], peer_keys[l],
                              peer_u[l], peer_v[l], ln2_g[l], ln2_b[l])
        return x

    y_prompt = trunk(x_prompt)
    y_sample = trunk(x_sample)
    return (y_prompt, y_sample)
```

```python
import functools
import math

import jax
import jax.numpy as jnp
from jax import lax
from jax.experimental import pallas as pl
from jax.experimental.pallas import tpu as pltpu

F32 = jnp.float32
BF16 = jnp.bfloat16
HI = lax.Precision.HIGHEST

LANES = 128
SHORT_K = 3
PEER_TOPK = 16
LN_EPS = 1e-5
DFT_N2 = 128
VMEM_LIMIT = 56 * 1024 * 1024


def _cp(sem, vmem=VMEM_LIMIT):
    return pltpu.CompilerParams(dimension_semantics=sem, vmem_limit_bytes=vmem)


def _in_proj_kernel(x_ref, w_ref, b_ref, o_ref):
    o_ref[...] = jnp.dot(x_ref[...].astype(BF16), w_ref[...],
                         preferred_element_type=F32) + b_ref[...]


def _in_proj(x2, w_bf, b):
    m, d = x2.shape
    n = w_bf.shape[1]
    tm = min(512, m)
    return pl.pallas_call(
        _in_proj_kernel,
        grid=(m // tm,),
        in_specs=[pl.BlockSpec((tm, d), lambda i: (i, 0)),
                  pl.BlockSpec((d, n), lambda i: (0, 0)),
                  pl.BlockSpec((1, n), lambda i: (0, 0))],
        out_specs=pl.BlockSpec((tm, n), lambda i: (i, 0)),
        out_shape=jax.ShapeDtypeStruct((m, n), F32),
        compiler_params=_cp(("parallel",)),
        name="in_proj",
    )(x2, w_bf, b)


def _mix_kernel(pm_ref, pp_ref, pn_ref, aw_ref, hw_ref, hb_ref, ya_ref, x0_ref, z_ref, *, cw):
    i = pl.program_id(1)
    n = pl.num_programs(1)
    tl = pm_ref.shape[1]
    rows = lax.broadcasted_iota(jnp.int32, (tl, 1), 0)
    has_prev = i > 0
    has_next = i < n - 1

    def conv3(cur, prev_row, next_row, w):
        prev_row = jnp.where(has_prev, prev_row, 0.0)
        next_row = jnp.where(has_next, next_row, 0.0)
        dn = jnp.where(rows == 0, prev_row, pltpu.roll(cur, 1, 0))
        up = jnp.where(rows == tl - 1, next_row, pltpu.roll(cur, tl - 1, 0))
        return dn * w[0:1] + cur * w[1:2] + up * w[2:3]

    def sec(ref, r0, r1, j):
        return ref[0, r0:r1, j * cw:(j + 1) * cw]

    g = sec(pm_ref, 0, tl, 1) * sec(pm_ref, 0, tl, 2)
    gp = sec(pp_ref, 7, 8, 1) * sec(pp_ref, 7, 8, 2)
    gn = sec(pn_ref, 0, 1, 1) * sec(pn_ref, 0, 1, 2)
    ya_ref[0] = sec(pm_ref, 0, tl, 0) * conv3(g, gp, gn, aw_ref[...])

    def hconv(j):
        c = 3 + j
        w = hw_ref[:, j * cw:(j + 1) * cw]
        return conv3(sec(pm_ref, 0, tl, c), sec(pp_ref, 7, 8, c), sec(pn_ref, 0, 1, c), w) \
            + hb_ref[:, j * cw:(j + 1) * cw]

    x0_ref[0] = hconv(0)
    z_ref[0] = hconv(2) * hconv(1)


def _mix(p3, a_conv_w, h_conv_w, h_conv_b, cw):
    b, l, pw = p3.shape
    tl = min(512, l)
    nl = l // tl
    r8 = tl // 8
    out = jax.ShapeDtypeStruct((b, l, cw), F32)
    ospec = pl.BlockSpec((1, tl, cw), lambda bi, i: (bi, i, 0))
    return pl.pallas_call(
        functools.partial(_mix_kernel, cw=cw),
        grid=(b, nl),
        in_specs=[pl.BlockSpec((1, tl, pw), lambda bi, i: (bi, i, 0)),
                  pl.BlockSpec((1, 8, pw), lambda bi, i: (bi, jnp.maximum(i * r8 - 1, 0), 0)),
                  pl.BlockSpec((1, 8, pw), lambda bi, i: (bi, jnp.minimum((i + 1) * r8, l // 8 - 1), 0)),
                  pl.BlockSpec((SHORT_K, cw), lambda bi, i: (0, 0)),
                  pl.BlockSpec((SHORT_K, 3 * cw), lambda bi, i: (0, 0)),
                  pl.BlockSpec((1, 3 * cw), lambda bi, i: (0, 0))],
        out_specs=[ospec, ospec, ospec],
        out_shape=[out, out, out],
        compiler_params=_cp(("parallel", "parallel")),
        name="mix",
    )(p3, p3, p3, a_conv_w, h_conv_w, h_conv_b)


def _filter_kernel(t_ref, w_ref, f_ref, w1t_ref, w1c_ref, w1s_ref, b1_ref, fr_ref, w2_ref, b2_ref,
                   w3_ref, dec_ref, hk_ref, asum_ref, *, hw):
    i = pl.program_id(0)
    tl = t_ref.shape[0]
    t = t_ref[...]
    ang = w_ref[...] * f_ref[...]
    z1 = (t * w1t_ref[...]
          + jnp.dot(jnp.cos(ang), w1c_ref[...], precision=HI, preferred_element_type=F32)
          + jnp.dot(-jnp.sin(ang), w1s_ref[...], precision=HI, preferred_element_type=F32)
          + b1_ref[...])
    fr = fr_ref[...]
    h = jnp.sin(fr * z1)
    h = jnp.sin(fr * (jnp.dot(h, w2_ref[...], precision=HI, preferred_element_type=F32) + b2_ref[...]))
    h = jnp.dot(h, w3_ref[...], precision=HI, preferred_element_type=F32)
    h = h * jnp.exp(-t * jnp.abs(dec_ref[...]))
    rows = i * tl + lax.broadcasted_iota(jnp.int32, (tl, 1), 0)
    fwd = h[:, :hw]
    bwd = jnp.where(rows == 0, 0.0, h[:, hw:])
    hk_ref[0] = fwd
    hk_ref[1] = bwd
    s = jnp.concatenate([jnp.sum(jnp.abs(fwd), axis=0, keepdims=True),
                         jnp.sum(jnp.abs(bwd), axis=0, keepdims=True)], axis=0)

    @pl.when(i == 0)
    def _():
        asum_ref[...] = jnp.zeros_like(asum_ref)

    asum_ref[...] += s


def _pad2(a, r, c):
    return jnp.zeros((r, c), F32).at[:a.shape[0], :a.shape[1]].set(a.astype(F32))


def _hyena_filter(l, hf_w1, hf_b1, hf_freq, hf_w2, hf_b2, hf_w3, hy_decay):
    bands = (hf_w1.shape[0] - 1) // 2
    hid = hf_w1.shape[1]
    hw = hy_decay.shape[1]
    t = jnp.linspace(0.0, 1.0, l, dtype=F32)[:, None]
    w = (2.0 * math.pi) * jnp.arange(l, dtype=F32)[:, None] / l
    f = jnp.linspace(1e-4, bands - 1, bands, dtype=F32)[None, :]
    p = LANES
    tl = min(512, l)
    args = (t, w, _pad2(f, 1, p), _pad2(hf_w1[0:1], 1, p), _pad2(hf_w1[1:1 + bands], p, p),
            _pad2(hf_w1[1 + bands:], p, p), _pad2(hf_b1[None], 1, p), _pad2(hf_freq[None], 1, p),
            _pad2(hf_w2, p, p), _pad2(hf_b2[None], 1, p), _pad2(hf_w3, p, 2 * hw),
            hy_decay.reshape(1, 2 * hw).astype(F32))
    del hid
    const = lambda shape: pl.BlockSpec(shape, lambda i: (0, 0))
    return pl.pallas_call(
        functools.partial(_filter_kernel, hw=hw),
        grid=(l // tl,),
        in_specs=[pl.BlockSpec((tl, 1), lambda i: (i, 0)), pl.BlockSpec((tl, 1), lambda i: (i, 0)),
                  const((1, p)), const((1, p)), const((p, p)), const((p, p)), const((1, p)),
                  const((1, p)), const((p, p)), const((1, p)), const((p, 2 * hw)), const((1, 2 * hw))],
        out_specs=[pl.BlockSpec((2, tl, hw), lambda i: (0, i, 0)),
                   pl.BlockSpec((2, hw), lambda i: (0, 0))],
        out_shape=[jax.ShapeDtypeStruct((2, l, hw), F32), jax.ShapeDtypeStruct((2, hw), F32)],
        compiler_params=_cp(("arbitrary",)),
        name="hyena_filter",
    )(*args)


def _dft_consts(l):
    n2s = DFT_N2
    n = 2 * l
    n1s = n // n2s
    two_pi = 2.0 * math.pi
    k1 = jnp.arange(n1s, dtype=jnp.int32)[:, None]
    n1 = jnp.arange(n1s // 2, dtype=jnp.int32)[None, :]
    ang = -two_pi * ((k1 * n1) % n1s).astype(F32) / n1s
    fa = jnp.concatenate([jnp.cos(ang), jnp.sin(ang)], axis=0)
    k1 = jnp.arange(n1s, dtype=jnp.int32)[:, None, None]
    k2 = jnp.arange(n2s, dtype=jnp.int32)[None, :, None]
    n2 = jnp.arange(n2s, dtype=jnp.int32)[None, None, :]
    ang = -two_pi * ((n2 * k2 * n1s + n2 * k1) % n).astype(F32) / n
    mr, mi = jnp.cos(ang), jnp.sin(ang)
    g = jnp.concatenate([jnp.concatenate([mr, -mi], 2), jnp.concatenate([mi, mr], 2)], 1)
    mrt, mit = jnp.swapaxes(mr, 1, 2), jnp.swapaxes(mi, 1, 2)
    gi = jnp.concatenate([jnp.concatenate([mrt, mit], 2), jnp.concatenate([-mit, mrt], 2)], 1)
    n1 = jnp.arange(n1s // 2, dtype=jnp.int32)[:, None]
    k1 = jnp.arange(n1s, dtype=jnp.int32)[None, :]
    ang = two_pi * ((n1 * k1) % n1s).astype(F32) / n1s
    fb = jnp.concatenate([jnp.cos(ang), -jnp.sin(ang)], axis=1) / n
    return fa, g, gi, fb


def _lmat_kernel(f_ref, z_ref, o_ref):
    o_ref[0] = jnp.dot(f_ref[...], z_ref[0], precision=HI, preferred_element_type=F32)


def _dft_stage1(fa, z3):
    b, r, cols = z3.shape
    m = fa.shape[0]
    tn = min(4096, cols)
    return pl.pallas_call(
        _lmat_kernel,
        grid=(b, cols // tn),
        in_specs=[pl.BlockSpec((m, r), lambda bi, j: (0, 0)),
                  pl.BlockSpec((1, r, tn), lambda bi, j: (bi, 0, j))],
        out_specs=pl.BlockSpec((1, m, tn), lambda bi, j: (bi, 0, j)),
        out_shape=jax.ShapeDtypeStruct((b, m, cols), F32),
        compiler_params=_cp(("parallel", "parallel")),
        name="dft_stage1",
    )(fa, z3)


def _kf_kernel(a_ref, g_ref, asum_ref, kf_ref):
    n2 = a_ref.shape[3]
    c = a_ref.shape[4]
    g = g_ref[0]
    f0 = jnp.dot(g, a_ref[0, :, 0].reshape(2 * n2, c), precision=HI, preferred_element_type=F32)
    f1 = jnp.dot(g, a_ref[1, :, 0].reshape(2 * n2, c), precision=HI, preferred_element_type=F32)
    inv = 1.0 / (asum_ref[0:1] + asum_ref[1:2])
    kf_ref[0] = jnp.concatenate([(f0[:n2] + f1[:n2]) * inv, (f0[n2:] - f1[n2:]) * inv], axis=0)


def _filter_spectrum(a5, g, asum):
    _, _, n1s, n2, c = a5.shape
    return pl.pallas_call(
        _kf_kernel,
        grid=(n1s,),
        in_specs=[pl.BlockSpec((2, 2, 1, n2, c), lambda k: (0, 0, k, 0, 0)),
                  pl.BlockSpec((1, 2 * n2, 2 * n2), lambda k: (k, 0, 0)),
                  pl.BlockSpec((2, c), lambda k: (0, 0))],
        out_specs=pl.BlockSpec((1, 2 * n2, c), lambda k: (k, 0, 0)),
        out_shape=jax.ShapeDtypeStruct((n1s, 2 * n2, c), F32),
        compiler_params=_cp(("parallel",)),
        name="filter_spectrum",
    )(a5, g, asum)


def _spec_kernel(a_ref, g_ref, kf_ref, gi_ref, d_ref):
    n2 = a_ref.shape[3]
    c = a_ref.shape[4]
    a = a_ref[0, :, 0].reshape(2 * n2, c)
    xf = jnp.dot(g_ref[0], a, precision=HI, preferred_element_type=F32)
    kf = kf_ref[0]
    xr, xi, kr, ki = xf[:n2], xf[n2:], kf[:n2], kf[n2:]
    y = jnp.concatenate([xr * kr - xi * ki, xr * ki + xi * kr], axis=0)
    d = jnp.dot(gi_ref[0], y, precision=HI, preferred_element_type=F32)
    d_ref[0, :, 0] = d.reshape(2, n2, c)


def _spectral_multiply(a5, g, kf, gi):
    b, _, n1s, n2, c = a5.shape
    blk = pl.BlockSpec((1, 2, 1, n2, c), lambda k, bi: (bi, 0, k, 0, 0))
    mat = pl.BlockSpec((1, 2 * n2, 2 * n2), lambda k, bi: (k, 0, 0))
    return pl.pallas_call(
        _spec_kernel,
        grid=(n1s, b),
        in_specs=[blk, mat, pl.BlockSpec((1, 2 * n2, c), lambda k, bi: (k, 0, 0)), mat],
        out_specs=blk,
        out_shape=jax.ShapeDtypeStruct(a5.shape, F32),
        compiler_params=_cp(("parallel", "arbitrary")),
        name="spectral_multiply",
    )(a5, g, kf, gi)


def _inv_kernel(f_ref, d_ref, z_ref, x0_ref, bias_ref, o_ref):
    y = jnp.dot(f_ref[...], d_ref[0], precision=HI, preferred_element_type=F32)
    o_ref[0] = x0_ref[0] * (y + z_ref[0] * bias_ref[...])


def _dft_final(fb, d3, z3, x03, bias_t):
    b, m2, cols = d3.shape
    r = fb.shape[0]
    tn = min(4096, cols)
    rspec = pl.BlockSpec((1, r, tn), lambda bi, j: (bi, 0, j))
    return pl.pallas_call(
        _inv_kernel,
        grid=(b, cols // tn),
        in_specs=[pl.BlockSpec((r, m2), lambda bi, j: (0, 0)),
                  pl.BlockSpec((1, m2, tn), lambda bi, j: (bi, 0, j)),
                  rspec, rspec,
                  pl.BlockSpec((1, tn), lambda bi, j: (0, j))],
        out_specs=rspec,
        out_shape=jax.ShapeDtypeStruct((b, r, cols), F32),
        compiler_params=_cp(("parallel", "parallel")),
        name="dft_final",
    )(fb, d3, z3, x03, bias_t)


def _hyena_mix(z, x0, hy_bias, hf):
    b, l, c = z.shape
    n2 = DFT_N2
    n1s = 2 * l // n2
    fa, g, gi, fb = _dft_consts(l)
    hk, asum = _hyena_filter(l, *hf)
    cols = n2 * c
    ka = _dft_stage1(fa, hk.reshape(2, n1s // 2, cols))
    kf = _filter_spectrum(ka.reshape(2, 2, n1s, n2, c), g, asum)
    z3 = z.reshape(b, n1s // 2, cols)
    a = _dft_stage1(fa, z3)
    d = _spectral_multiply(a.reshape(b, 2, n1s, n2, c), g, kf, gi)
    bias_t = jnp.tile(hy_bias.astype(F32), n2)[None, :]
    yh = _dft_final(fb, d.reshape(b, 2 * n1s, cols), z3, x0.reshape(b, n1s // 2, cols), bias_t)
    return yh.reshape(b, l, c)


def _layer_norm(r, g, b):
    mu = jnp.mean(r, axis=-1, keepdims=True)
    rc = r - mu
    var = jnp.mean(rc * rc, axis=-1, keepdims=True)
    return rc * lax.rsqrt(var + LN_EPS) * g + b


def _out_proj_kernel(ya_ref, yh_ref, x_ref, wa_ref, wh_ref, g_ref, b_ref, o_ref, *, alpha):
    mix = (jnp.dot(ya_ref[...].astype(BF16), wa_ref[...], preferred_element_type=F32)
           + jnp.dot(yh_ref[...].astype(BF16), wh_ref[...], preferred_element_type=F32))
    o_ref[...] = _layer_norm(alpha * x_ref[...] + mix, g_ref[...], b_ref[...])


def _out_proj(ya, yh, x2, wa_bf, wh_bf, g, b, alpha):
    m, d = x2.shape
    cw = ya.shape[1]
    tm = min(512, m)
    row = lambda w: pl.BlockSpec((tm, w), lambda i: (i, 0))
    const = lambda shape: pl.BlockSpec(shape, lambda i: (0, 0))
    return pl.pallas_call(
        functools.partial(_out_proj_kernel, alpha=alpha),
        grid=(m // tm,),
        in_specs=[row(cw), row(cw), row(d), const((cw, d)), const((cw, d)), const((1, d)), const((1, d))],
        out_specs=row(d),
        out_shape=jax.ShapeDtypeStruct((m, d), F32),
        compiler_params=_cp(("parallel",)),
        name="out_proj_ln",
    )(ya, yh, x2, wa_bf, wh_bf, g, b)


def _topk_rows(s, k, payload=None):
    n = s.shape[0]
    iota = lax.broadcasted_iota(jnp.int32, s.shape, 0).astype(F32)
    vals, poss, pays = [], [], []
    for _ in range(k):
        m = jnp.max(s, axis=0, keepdims=True)
        pos = jnp.min(jnp.where(s == m, iota, float(n)), axis=0, keepdims=True)
        hit = iota == pos
        vals.append(m)
        poss.append(pos)
        if payload is not None:
            pays.append(jnp.max(jnp.where(hit, payload, -1.0), axis=0, keepdims=True))
        s = jnp.where(hit, -jnp.inf, s)
    cat = lambda xs: jnp.concatenate(xs, axis=0)
    return cat(vals), cat(poss), (cat(pays) if payload is not None else None)


def _route_kernel(x_ref, wq_ref, keys_ref, eidx_ref, gate_ref, q_scr, g_scr, *, heads, nkeys):
    kk = PEER_TOPK
    half = keys_ref.shape[2]
    q_scr[...] = jnp.dot(x_ref[...].astype(BF16), wq_ref[...], preferred_element_type=F32)

    def head(h, carry):
        tops = []
        for p in range(2):
            c = h * 2 + p
            qc = q_scr[:, pl.ds(pl.multiple_of(c * half, half), half)].astype(BF16)
            st = lax.dot_general(keys_ref[c], qc, (((1,), (1,)), ((), ())),
                                 preferred_element_type=F32)
            v, i, _ = _topk_rows(st, kk)
            tops.append((v, i))
        (v1, i1), (v2, i2) = tops
        cand = jnp.concatenate([v1[a:a + 1] + v2 for a in range(kk)], axis=0)
        cidx = jnp.concatenate([i1[a:a + 1] * float(nkeys) + i2 for a in range(kk)], axis=0)
        best, _, e = _topk_rows(cand, kk, payload=cidx)
        ex = jnp.exp(best - jnp.max(best, axis=0, keepdims=True))
        gate = ex / jnp.sum(ex, axis=0, keepdims=True)
        r0 = pl.multiple_of(h * kk, kk)
        eidx_ref[pl.ds(r0, kk), :] = e.astype(jnp.int32)
        g_scr[pl.ds(r0, kk), :] = gate
        return carry

    lax.fori_loop(0, heads, head, 0)
    gate_ref[...] = g_scr[...].T


def _route(x2, wq_bf, keys_bf, heads, nkeys):
    m, d = x2.shape
    qd = wq_bf.shape[1]
    tt = min(256, m)
    hk = heads * PEER_TOPK
    return pl.pallas_call(
        functools.partial(_route_kernel, heads=heads, nkeys=nkeys),
        grid=(m // tt,),
        in_specs=[pl.BlockSpec((tt, d), lambda i: (i, 0)),
                  pl.BlockSpec((d, qd), lambda i: (0, 0)),
                  pl.BlockSpec(keys_bf.shape, lambda i: (0, 0, 0))],
        out_specs=[pl.BlockSpec((hk, tt), lambda i: (0, i)),
                   pl.BlockSpec((tt, hk), lambda i: (i, 0))],
        out_shape=[jax.ShapeDtypeStruct((hk, m), jnp.int32), jax.ShapeDtypeStruct((m, hk), F32)],
        scratch_shapes=[pltpu.VMEM((tt, qd), F32), pltpu.VMEM((hk, tt), F32)],
        compiler_params=_cp(("parallel",)),
        name="peer_route",
    )(x2, wq_bf, keys_bf)


def _rowsum_to_lanes(m):
    ones = jnp.ones((8, m.shape[1]), BF16)
    acc = jnp.zeros((8, m.shape[0]), F32)
    r = m
    for _ in range(3):
        hi = r.astype(BF16)
        r = r - hi.astype(F32)
        acc = acc + lax.dot_general(ones, hi, (((1,), (1,)), ((), ())), preferred_element_type=F32)
    return acc[0:1]


def _peer_u_kernel(idx_ref, x_ref, u_ref, o_ref):
    nk, tt = idx_ref.shape

    def tok(t, carry):
        xt = x_ref[t]
        rows = [jnp.sum(u_ref[idx_ref[k, t]] * xt, axis=0, keepdims=True) for k in range(nk)]
        o_ref[0, pl.ds(t, 1), :] = _rowsum_to_lanes(jnp.concatenate(rows, axis=0))
        return carry

    lax.fori_loop(0, tt, tok, 0)


def _peer_u(eidx_t, x4, u4):
    nk, m = eidx_t.shape
    ne, ns, sub, ln = u4.shape
    tt = min(128, m)
    return pl.pallas_call(
        _peer_u_kernel,
        grid=(ns, m // tt),
        in_specs=[pl.BlockSpec((nk, tt), lambda s, i: (0, i), memory_space=pltpu.SMEM),
                  pl.BlockSpec((tt, None, sub, ln), lambda s, i: (i, s, 0, 0)),
                  pl.BlockSpec((ne, None, sub, ln), lambda s, i: (0, s, 0, 0),
                               pipeline_mode=pl.Buffered(1))],
        out_specs=pl.BlockSpec((1, tt, nk), lambda s, i: (s, i, 0)),
        out_shape=jax.ShapeDtypeStruct((ns, m, nk), F32),
        compiler_params=_cp(("arbitrary", "arbitrary")),
        name="peer_u",
    )(eidx_t, x4, u4)


def _coef_kernel(act_ref, gate_ref, o_ref):
    a = jnp.sum(act_ref[...], axis=0)
    gelu = 0.5 * a * (1.0 + lax.erf(a * (1.0 / math.sqrt(2.0))))
    o_ref[...] = gate_ref[...] * gelu


def _coef(act, gate):
    ns, m, nk = act.shape
    tt = min(1024, m)
    return pl.pallas_call(
        _coef_kernel,
        grid=(m // tt,),
        in_specs=[pl.BlockSpec((ns, tt, nk), lambda i: (0, i, 0)),
                  pl.BlockSpec((tt, nk), lambda i: (i, 0))],
        out_specs=pl.BlockSpec((tt, nk), lambda i: (i, 0)),
        out_shape=jax.ShapeDtypeStruct((m, nk), F32),
        compiler_params=_cp(("parallel",)),
        name="peer_coef",
    )(act, gate)


def _peer_v_kernel(idx_ref, coef_ref, v_ref, o_ref):
    nk, tt = idx_ref.shape
    sub, ln = v_ref.shape[1], v_ref.shape[2]
    nacc = 4

    def tok(t, carry):
        accs = [jnp.zeros((sub, ln), F32) for _ in range(nacc)]
        for k in range(nk):
            accs[k % nacc] = accs[k % nacc] + v_ref[idx_ref[k, t]] * coef_ref[t, k]
        o_ref[t] = (accs[0] + accs[1]) + (accs[2] + accs[3])
        return carry

    lax.fori_loop(0, tt, tok, 0)


def _peer_v(eidx_t, coef, v4):
    nk, m = eidx_t.shape
    ne, ns, sub, ln = v4.shape
    tt = min(128, m)
    return pl.pallas_call(
        _peer_v_kernel,
        grid=(ns, m // tt),
        in_specs=[pl.BlockSpec((nk, tt), lambda s, i: (0, i), memory_space=pltpu.SMEM),
                  pl.BlockSpec((tt, nk), lambda s, i: (i, 0), memory_space=pltpu.SMEM),
                  pl.BlockSpec((ne, None, sub, ln), lambda s, i: (0, s, 0, 0),
                               pipeline_mode=pl.Buffered(1))],
        out_specs=pl.BlockSpec((tt, None, sub, ln), lambda s, i: (i, s, 0, 0)),
        out_shape=jax.ShapeDtypeStruct((m, ns, sub, ln), F32),
        compiler_params=_cp(("arbitrary", "arbitrary")),
        name="peer_v",
    )(eidx_t, coef, v4)


def _res_ln_kernel(x_ref, y_ref, g_ref, b_ref, o_ref, *, alpha):
    o_ref[...] = _layer_norm(alpha * x_ref[...] + y_ref[...], g_ref[...], b_ref[...])


def _res_ln(x2, y2, g, b, alpha):
    m, d = x2.shape
    tm = min(1024, m)
    row = pl.BlockSpec((tm, d), lambda i: (i, 0))
    const = pl.BlockSpec((1, d), lambda i: (0, 0))
    return pl.pallas_call(
        functools.partial(_res_ln_kernel, alpha=alpha),
        grid=(m // tm,),
        in_specs=[row, row, const, const],
        out_specs=row,
        out_shape=jax.ShapeDtypeStruct((m, d), F32),
        compiler_params=_cp(("parallel",)),
        name="residual_ln",
    )(x2, y2, g, b)


PEER_SLICES = 2


def _peer(x2, wq_bf, keys_bf, u4, v4, heads, nkeys):
    m, d = x2.shape
    ns, sub, ln = u4.shape[1:]
    eidx_t, gate = _route(x2, wq_bf, keys_bf, heads, nkeys)
    act = _peer_u(eidx_t, x2.reshape(m, ns, sub, ln), u4)
    coef = _coef(act, gate)
    return _peer_v(eidx_t, coef, v4).reshape(m, d)


def _encoder_layer(x, lw, alpha):
    b, l, d = x.shape
    cw = lw["a_conv_w"].shape[1]
    x2 = x.reshape(b * l, d)
    p = _in_proj(x2, lw["w_in"], lw["b_in"])
    ya, x0, z = _mix(p.reshape(b, l, -1), lw["a_conv_w"], lw["h_conv_w"], lw["h_conv_b"], cw)
    yh = _hyena_mix(z, x0, lw["hy_bias"], lw["hf"])
    x1 = _out_proj(ya.reshape(b * l, cw), yh.reshape(b * l, -1), x2, lw["w_out_a"], lw["w_out_h"],
                   lw["ln1_g"], lw["ln1_b"], alpha)
    y = _peer(x1, lw["peer_wq"], lw["peer_keys"], lw["peer_u"], lw["peer_v"], lw["heads"], lw["nkeys"])
    return _res_ln(x1, y, lw["ln2_g"], lw["ln2_b"], alpha).reshape(b, l, d)


def kernel(x_prompt, x_sample, w_in, b_in, a_conv_w, h_conv_w, h_conv_b, hf_w1, hf_b1, hf_freq, hf_w2,
           hf_b2, hf_w3, hy_decay, hy_bias, w_out, ln1_g, ln1_b, peer_wq, peer_keys, peer_u, peer_v,
           ln2_g, ln2_b):
    depth = w_in.shape[0]
    alpha = (2.0 * depth) ** 0.25
    d = x_prompt.shape[-1]
    sub = d // (PEER_SLICES * LANES)
    layers = []
    for i in range(depth):
        cw = a_conv_w.shape[2]
        heads, _, nkeys, half = peer_keys.shape[1:]
        ne = peer_u.shape[1]
        layers.append(dict(
            w_in=w_in[i].astype(BF16), b_in=b_in[i][None].astype(F32),
            a_conv_w=a_conv_w[i], h_conv_w=h_conv_w[i], h_conv_b=h_conv_b[i][None],
            hf=(hf_w1[i], hf_b1[i], hf_freq[i], hf_w2[i], hf_b2[i], hf_w3[i], hy_decay[i]),
            hy_bias=hy_bias[i],
            w_out_a=w_out[i, :cw].astype(BF16), w_out_h=w_out[i, cw:].astype(BF16),
            ln1_g=ln1_g[i][None], ln1_b=ln1_b[i][None],
            peer_wq=peer_wq[i].astype(BF16),
            peer_keys=peer_keys[i].reshape(heads * 2, nkeys, half).astype(BF16),
            peer_u=peer_u[i].reshape(ne, PEER_SLICES, sub, LANES),
            peer_v=peer_v[i].reshape(ne, PEER_SLICES, sub, LANES),
            ln2_g=ln2_g[i][None], ln2_b=ln2_b[i][None], heads=heads, nkeys=nkeys))

    def trunk(x):
        for lw in layers:
            x = _encoder_layer(x, lw, alpha)
        return x

    return trunk(x_prompt), trunk(x_sample)
```

```python
import functools
import math

import jax
import jax.numpy as jnp
from jax import lax
from jax.experimental import pallas as pl
from jax.experimental.pallas import tpu as pltpu

F32 = jnp.float32
BF16 = jnp.bfloat16
HI = lax.Precision.HIGHEST

LANES = 128
SHORT_K = 3
PEER_TOPK = 16
LN_EPS = 1e-5
DFT_N2 = 128
VMEM_LIMIT = 56 * 1024 * 1024


def _cp(sem, vmem=VMEM_LIMIT):
    return pltpu.CompilerParams(dimension_semantics=sem, vmem_limit_bytes=vmem)


def _in_proj_kernel(x_ref, w_ref, b_ref, o_ref):
    o_ref[...] = jnp.dot(x_ref[...].astype(BF16), w_ref[...],
                         preferred_element_type=F32) + b_ref[...]


def _in_proj(x2, w_bf, b):
    m, d = x2.shape
    n = w_bf.shape[1]
    tm = min(512, m)
    return pl.pallas_call(
        _in_proj_kernel,
        grid=(m // tm,),
        in_specs=[pl.BlockSpec((tm, d), lambda i: (i, 0)),
                  pl.BlockSpec((d, n), lambda i: (0, 0)),
                  pl.BlockSpec((1, n), lambda i: (0, 0))],
        out_specs=pl.BlockSpec((tm, n), lambda i: (i, 0)),
        out_shape=jax.ShapeDtypeStruct((m, n), F32),
        compiler_params=_cp(("parallel",)),
        name="in_proj",
    )(x2, w_bf, b)


def _mix_kernel(pm_ref, pp_ref, pn_ref, aw_ref, hw_ref, hb_ref, ya_ref, x0_ref, z_ref, *, cw):
    i = pl.program_id(1)
    n = pl.num_programs(1)
    tl = pm_ref.shape[1]
    rows = lax.broadcasted_iota(jnp.int32, (tl, 1), 0)
    has_prev = i > 0
    has_next = i < n - 1

    def conv3(cur, prev_row, next_row, w):
        prev_row = jnp.where(has_prev, prev_row, 0.0)
        next_row = jnp.where(has_next, next_row, 0.0)
        dn = jnp.where(rows == 0, prev_row, pltpu.roll(cur, 1, 0))
        up = jnp.where(rows == tl - 1, next_row, pltpu.roll(cur, tl - 1, 0))
        return dn * w[0:1] + cur * w[1:2] + up * w[2:3]

    def sec(ref, r0, r1, j):
        return ref[0, r0:r1, j * cw:(j + 1) * cw]

    g = sec(pm_ref, 0, tl, 1) * sec(pm_ref, 0, tl, 2)
    gp = sec(pp_ref, 7, 8, 1) * sec(pp_ref, 7, 8, 2)
    gn = sec(pn_ref, 0, 1, 1) * sec(pn_ref, 0, 1, 2)
    ya_ref[0] = sec(pm_ref, 0, tl, 0) * conv3(g, gp, gn, aw_ref[...])

    def hconv(j):
        c = 3 + j
        w = hw_ref[:, j * cw:(j + 1) * cw]
        return conv3(sec(pm_ref, 0, tl, c), sec(pp_ref, 7, 8, c), sec(pn_ref, 0, 1, c), w) \
            + hb_ref[:, j * cw:(j + 1) * cw]

    x0_ref[0] = hconv(0)
    z_ref[0] = hconv(2) * hconv(1)


def _mix(p3, a_conv_w, h_conv_w, h_conv_b, cw):
    b, l, pw = p3.shape
    tl = min(512, l)
    nl = l // tl
    r8 = tl // 8
    out = jax.ShapeDtypeStruct((b, l, cw), F32)
    ospec = pl.BlockSpec((1, tl, cw), lambda bi, i: (bi, i, 0))
    return pl.pallas_call(
        functools.partial(_mix_kernel, cw=cw),
        grid=(b, nl),
        in_specs=[pl.BlockSpec((1, tl, pw), lambda bi, i: (bi, i, 0)),
                  pl.BlockSpec((1, 8, pw), lambda bi, i: (bi, jnp.maximum(i * r8 - 1, 0), 0)),
                  pl.BlockSpec((1, 8, pw), lambda bi, i: (bi, jnp.minimum((i + 1) * r8, l // 8 - 1), 0)),
                  pl.BlockSpec((SHORT_K, cw), lambda bi, i: (0, 0)),
                  pl.BlockSpec((SHORT_K, 3 * cw), lambda bi, i: (0, 0)),
                  pl.BlockSpec((1, 3 * cw), lambda bi, i: (0, 0))],
        out_specs=[ospec, ospec, ospec],
        out_shape=[out, out, out],
        compiler_params=_cp(("parallel", "parallel")),
        name="mix",
    )(p3, p3, p3, a_conv_w, h_conv_w, h_conv_b)


def _filter_kernel(t_ref, w_ref, f_ref, w1t_ref, w1c_ref, w1s_ref, b1_ref, fr_ref, w2_ref, b2_ref,
                   w3_ref, dec_ref, hk_ref, asum_ref, *, hw):
    i = pl.program_id(0)
    tl = t_ref.shape[0]
    t = t_ref[...]
    ang = w_ref[...] * f_ref[...]
    z1 = (t * w1t_ref[...]
          + jnp.dot(jnp.cos(ang), w1c_ref[...], precision=HI, preferred_element_type=F32)
          + jnp.dot(-jnp.sin(ang), w1s_ref[...], precision=HI, preferred_element_type=F32)
          + b1_ref[...])
    fr = fr_ref[...]
    h = jnp.sin(fr * z1)
    h = jnp.sin(fr * (jnp.dot(h, w2_ref[...], precision=HI, preferred_element_type=F32) + b2_ref[...]))
    h = jnp.dot(h, w3_ref[...], precision=HI, preferred_element_type=F32)
    h = h * jnp.exp(-t * jnp.abs(dec_ref[...]))
    rows = i * tl + lax.broadcasted_iota(jnp.int32, (tl, 1), 0)
    fwd = h[:, :hw]
    bwd = jnp.where(rows == 0, 0.0, h[:, hw:])
    hk_ref[0] = fwd
    hk_ref[1] = bwd
    s = jnp.concatenate([jnp.sum(jnp.abs(fwd), axis=0, keepdims=True),
                         jnp.sum(jnp.abs(bwd), axis=0, keepdims=True)], axis=0)

    @pl.when(i == 0)
    def _():
        asum_ref[...] = jnp.zeros_like(asum_ref)

    asum_ref[...] += s


def _pad2(a, r, c):
    return jnp.zeros((r, c), F32).at[:a.shape[0], :a.shape[1]].set(a.astype(F32))


def _hyena_filter(l, hf_w1, hf_b1, hf_freq, hf_w2, hf_b2, hf_w3, hy_decay):
    bands = (hf_w1.shape[0] - 1) // 2
    hid = hf_w1.shape[1]
    hw = hy_decay.shape[1]
    t = jnp.linspace(0.0, 1.0, l, dtype=F32)[:, None]
    w = (2.0 * math.pi) * jnp.arange(l, dtype=F32)[:, None] / l
    f = jnp.linspace(1e-4, bands - 1, bands, dtype=F32)[None, :]
    p = LANES
    tl = min(512, l)
    args = (t, w, _pad2(f, 1, p), _pad2(hf_w1[0:1], 1, p), _pad2(hf_w1[1:1 + bands], p, p),
            _pad2(hf_w1[1 + bands:], p, p), _pad2(hf_b1[None], 1, p), _pad2(hf_freq[None], 1, p),
            _pad2(hf_w2, p, p), _pad2(hf_b2[None], 1, p), _pad2(hf_w3, p, 2 * hw),
            hy_decay.reshape(1, 2 * hw).astype(F32))
    del hid
    const = lambda shape: pl.BlockSpec(shape, lambda i: (0, 0))
    return pl.pallas_call(
        functools.partial(_filter_kernel, hw=hw),
        grid=(l // tl,),
        in_specs=[pl.BlockSpec((tl, 1), lambda i: (i, 0)), pl.BlockSpec((tl, 1), lambda i: (i, 0)),
                  const((1, p)), const((1, p)), const((p, p)), const((p, p)), const((1, p)),
                  const((1, p)), const((p, p)), const((1, p)), const((p, 2 * hw)), const((1, 2 * hw))],
        out_specs=[pl.BlockSpec((2, tl, hw), lambda i: (0, i, 0)),
                   pl.BlockSpec((2, hw), lambda i: (0, 0))],
        out_shape=[jax.ShapeDtypeStruct((2, l, hw), F32), jax.ShapeDtypeStruct((2, hw), F32)],
        compiler_params=_cp(("arbitrary",)),
        name="hyena_filter",
    )(*args)


def _dft_consts(l):
    n2s = DFT_N2
    n = 2 * l
    n1s = n // n2s
    two_pi = 2.0 * math.pi
    k1 = jnp.arange(n1s, dtype=jnp.int32)[:, None]
    n1 = jnp.arange(n1s // 2, dtype=jnp.int32)[None, :]
    ang = -two_pi * ((k1 * n1) % n1s).astype(F32) / n1s
    fa = jnp.concatenate([jnp.cos(ang), jnp.sin(ang)], axis=0)
    k1 = jnp.arange(n1s, dtype=jnp.int32)[:, None, None]
    k2 = jnp.arange(n2s, dtype=jnp.int32)[None, :, None]
    n2 = jnp.arange(n2s, dtype=jnp.int32)[None, None, :]
    ang = -two_pi * ((n2 * k2 * n1s + n2 * k1) % n).astype(F32) / n
    mr, mi = jnp.cos(ang), jnp.sin(ang)
    g = jnp.concatenate([jnp.concatenate([mr, -mi], 2), jnp.concatenate([mi, mr], 2)], 1)
    mrt, mit = jnp.swapaxes(mr, 1, 2), jnp.swapaxes(mi, 1, 2)
    gi = jnp.concatenate([jnp.concatenate([mrt, mit], 2), jnp.concatenate([-mit, mrt], 2)], 1)
    n1 = jnp.arange(n1s // 2, dtype=jnp.int32)[:, None]
    k1 = jnp.arange(n1s, dtype=jnp.int32)[None, :]
    ang = two_pi * ((n1 * k1) % n1s).astype(F32) / n1s
    fb = jnp.concatenate([jnp.cos(ang), -jnp.sin(ang)], axis=1) / n
    return fa, g, gi, fb


def _lmat_kernel(f_ref, z_ref, o_ref):
    o_ref[0] = jnp.dot(f_ref[...], z_ref[0], precision=HI, preferred_element_type=F32)


def _dft_stage1(fa, z3):
    b, r, cols = z3.shape
    m = fa.shape[0]
    tn = min(4096, cols)
    return pl.pallas_call(
        _lmat_kernel,
        grid=(b, cols // tn),
        in_specs=[pl.BlockSpec((m, r), lambda bi, j: (0, 0)),
                  pl.BlockSpec((1, r, tn), lambda bi, j: (bi, 0, j))],
        out_specs=pl.BlockSpec((1, m, tn), lambda bi, j: (bi, 0, j)),
        out_shape=jax.ShapeDtypeStruct((b, m, cols), F32),
        compiler_params=_cp(("parallel", "parallel")),
        name="dft_stage1",
    )(fa, z3)


def _kf_kernel(a_ref, g_ref, asum_ref, kf_ref):
    n2 = a_ref.shape[3]
    c = a_ref.shape[4]
    g = g_ref[0]
    f0 = jnp.dot(g, a_ref[0, :, 0].reshape(2 * n2, c), precision=HI, preferred_element_type=F32)
    f1 = jnp.dot(g, a_ref[1, :, 0].reshape(2 * n2, c), precision=HI, preferred_element_type=F32)
    inv = 1.0 / (asum_ref[0:1] + asum_ref[1:2])
    kf_ref[0] = jnp.concatenate([(f0[:n2] + f1[:n2]) * inv, (f0[n2:] - f1[n2:]) * inv], axis=0)


def _filter_spectrum(a5, g, asum):
    _, _, n1s, n2, c = a5.shape
    return pl.pallas_call(
        _kf_kernel,
        grid=(n1s,),
        in_specs=[pl.BlockSpec((2, 2, 1, n2, c), lambda k: (0, 0, k, 0, 0)),
                  pl.BlockSpec((1, 2 * n2, 2 * n2), lambda k: (k, 0, 0)),
                  pl.BlockSpec((2, c), lambda k: (0, 0))],
        out_specs=pl.BlockSpec((1, 2 * n2, c), lambda k: (k, 0, 0)),
        out_shape=jax.ShapeDtypeStruct((n1s, 2 * n2, c), F32),
        compiler_params=_cp(("parallel",)),
        name="filter_spectrum",
    )(a5, g, asum)


def _spec_kernel(a_ref, g_ref, kf_ref, gi_ref, d_ref):
    n2 = a_ref.shape[3]
    c = a_ref.shape[4]
    a = a_ref[0, :, 0].reshape(2 * n2, c)
    xf = jnp.dot(g_ref[0], a, precision=HI, preferred_element_type=F32)
    kf = kf_ref[0]
    xr, xi, kr, ki = xf[:n2], xf[n2:], kf[:n2], kf[n2:]
    y = jnp.concatenate([xr * kr - xi * ki, xr * ki + xi * kr], axis=0)
    d = jnp.dot(gi_ref[0], y, precision=HI, preferred_element_type=F32)
    d_ref[0, :, 0] = d.reshape(2, n2, c)


def _spectral_multiply(a5, g, kf, gi):
    b, _, n1s, n2, c = a5.shape
    blk = pl.BlockSpec((1, 2, 1, n2, c), lambda k, bi: (bi, 0, k, 0, 0))
    mat = pl.BlockSpec((1, 2 * n2, 2 * n2), lambda k, bi: (k, 0, 0))
    return pl.pallas_call(
        _spec_kernel,
        grid=(n1s, b),
        in_specs=[blk, mat, pl.BlockSpec((1, 2 * n2, c), lambda k, bi: (k, 0, 0)), mat],
        out_specs=blk,
        out_shape=jax.ShapeDtypeStruct(a5.shape, F32),
        compiler_params=_cp(("parallel", "arbitrary")),
        name="spectral_multiply",
    )(a5, g, kf, gi)


def _inv_kernel(f_ref, d_ref, z_ref, x0_ref, bias_ref, o_ref):
    y = jnp.dot(f_ref[...], d_ref[0], precision=HI, preferred_element_type=F32)
    o_ref[0] = x0_ref[0] * (y + z_ref[0] * bias_ref[...])


def _dft_final(fb, d3, z3, x03, bias_t):
    b, m2, cols = d3.shape
    r = fb.shape[0]
    tn = min(4096, cols)
    rspec = pl.BlockSpec((1, r, tn), lambda bi, j: (bi, 0, j))
    return pl.pallas_call(
        _inv_kernel,
        grid=(b, cols // tn),
        in_specs=[pl.BlockSpec((r, m2), lambda bi, j: (0, 0)),
                  pl.BlockSpec((1, m2, tn), lambda bi, j: (bi, 0, j)),
                  rspec, rspec,
                  pl.BlockSpec((1, tn), lambda bi, j: (0, j))],
        out_specs=rspec,
        out_shape=jax.ShapeDtypeStruct((b, r, cols), F32),
        compiler_params=_cp(("parallel", "parallel")),
        name="dft_final",
    )(fb, d3, z3, x03, bias_t)


def _hyena_mix(z, x0, hy_bias, hf):
    b, l, c = z.shape
    n2 = DFT_N2
    n1s = 2 * l // n2
    fa, g, gi, fb = _dft_consts(l)
    hk, asum = _hyena_filter(l, *hf)
    cols = n2 * c
    ka = _dft_stage1(fa, hk.reshape(2, n1s // 2, cols))
    kf = _filter_spectrum(ka.reshape(2, 2, n1s, n2, c), g, asum)
    z3 = z.reshape(b, n1s // 2, cols)
    a = _dft_stage1(fa, z3)
    d = _spectral_multiply(a.reshape(b, 2, n1s, n2, c), g, kf, gi)
    bias_t = jnp.tile(hy_bias.astype(F32), n2)[None, :]
    yh = _dft_final(fb, d.reshape(b, 2 * n1s, cols), z3, x0.reshape(b, n1s // 2, cols), bias_t)
    return yh.reshape(b, l, c)


def _layer_norm(r, g, b):
    mu = jnp.mean(r, axis=-1, keepdims=True)
    rc = r - mu
    var = jnp.mean(rc * rc, axis=-1, keepdims=True)
    return rc * lax.rsqrt(var + LN_EPS) * g + b


def _out_proj_kernel(ya_ref, yh_ref, x_ref, wa_ref, wh_ref, g_ref, b_ref, o_ref, *, alpha):
    mix = (jnp.dot(ya_ref[...].astype(BF16), wa_ref[...], preferred_element_type=F32)
           + jnp.dot(yh_ref[...].astype(BF16), wh_ref[...], preferred_element_type=F32))
    o_ref[...] = _layer_norm(alpha * x_ref[...] + mix, g_ref[...], b_ref[...])


def _out_proj(ya, yh, x2, wa_bf, wh_bf, g, b, alpha):
    m, d = x2.shape
    cw = ya.shape[1]
    tm = min(512, m)
    row = lambda w: pl.BlockSpec((tm, w), lambda i: (i, 0))
    const = lambda shape: pl.BlockSpec(shape, lambda i: (0, 0))
    return pl.pallas_call(
        functools.partial(_out_proj_kernel, alpha=alpha),
        grid=(m // tm,),
        in_specs=[row(cw), row(cw), row(d), const((cw, d)), const((cw, d)), const((1, d)), const((1, d))],
        out_specs=row(d),
        out_shape=jax.ShapeDtypeStruct((m, d), F32),
        compiler_params=_cp(("parallel",)),
        name="out_proj_ln",
    )(ya, yh, x2, wa_bf, wh_bf, g, b)


def _topk_rows(s, k, payload=None):
    n = s.shape[0]
    iota = lax.broadcasted_iota(jnp.int32, s.shape, 0).astype(F32)
    vals, poss, pays = [], [], []
    for _ in range(k):
        m = jnp.max(s, axis=0, keepdims=True)
        pos = jnp.min(jnp.where(s == m, iota, float(n)), axis=0, keepdims=True)
        hit = iota == pos
        vals.append(m)
        poss.append(pos)
        if payload is not None:
            pays.append(jnp.max(jnp.where(hit, payload, -1.0), axis=0, keepdims=True))
        s = jnp.where(hit, -jnp.inf, s)
    cat = lambda xs: jnp.concatenate(xs, axis=0)
    return cat(vals), cat(poss), (cat(pays) if payload is not None else None)


def _route_kernel(x_ref, wq_ref, keys_ref, eidx_ref, gate_ref, q_scr, e_scr, g_scr, *, heads, nkeys):
    kk = PEER_TOPK
    half = keys_ref.shape[2]
    q_scr[...] = jnp.dot(x_ref[...].astype(BF16), wq_ref[...], preferred_element_type=F32)

    def head(h, carry):
        tops = []
        for p in range(2):
            c = h * 2 + p
            qc = q_scr[:, pl.ds(pl.multiple_of(c * half, half), half)].astype(BF16)
            st = lax.dot_general(keys_ref[c], qc, (((1,), (1,)), ((), ())),
                                 preferred_element_type=F32)
            v, i, _ = _topk_rows(st, kk)
            tops.append((v, i))
        (v1, i1), (v2, i2) = tops
        cand = jnp.concatenate([v1[a:a + 1] + v2 for a in range(kk)], axis=0)
        cidx = jnp.concatenate([i1[a:a + 1] * float(nkeys) + i2 for a in range(kk)], axis=0)
        best, _, e = _topk_rows(cand, kk, payload=cidx)
        ex = jnp.exp(best - jnp.max(best, axis=0, keepdims=True))
        gate = ex / jnp.sum(ex, axis=0, keepdims=True)
        r0 = pl.multiple_of(h * kk, kk)
        e_scr[pl.ds(r0, kk), :] = e
        g_scr[pl.ds(r0, kk), :] = gate
        return carry

    lax.fori_loop(0, heads, head, 0)
    eidx_ref[...] = e_scr[...].T.astype(jnp.int32)
    gate_ref[...] = g_scr[...].T


def _route(x2, wq_bf, keys_bf, heads, nkeys):
    m, d = x2.shape
    qd = wq_bf.shape[1]
    tt = min(256, m)
    hk = heads * PEER_TOPK
    return pl.pallas_call(
        functools.partial(_route_kernel, heads=heads, nkeys=nkeys),
        grid=(m // tt,),
        in_specs=[pl.BlockSpec((tt, d), lambda i: (i, 0)),
                  pl.BlockSpec((d, qd), lambda i: (0, 0)),
                  pl.BlockSpec(keys_bf.shape, lambda i: (0, 0, 0))],
        out_specs=[pl.BlockSpec((tt, hk), lambda i: (i, 0)),
                   pl.BlockSpec((tt, hk), lambda i: (i, 0))],
        out_shape=[jax.ShapeDtypeStruct((m, hk), jnp.int32), jax.ShapeDtypeStruct((m, hk), F32)],
        scratch_shapes=[pltpu.VMEM((tt, qd), F32), pltpu.VMEM((hk, tt), F32), pltpu.VMEM((hk, tt), F32)],
        compiler_params=_cp(("parallel",)),
        name="peer_route",
    )(x2, wq_bf, keys_bf)


SUBLANES = 8


def _pack_table(tab):
    e, d = tab.shape
    bits = lax.bitcast_convert_type(tab.astype(BF16).reshape(e, d // (2 * LANES), 2, LANES), jnp.uint16)
    bits = bits.astype(jnp.uint32)
    return lax.bitcast_convert_type(bits[:, :, 0, :] | (bits[:, :, 1, :] << 16), jnp.int32)


def _gather_rows(tab_ref, idx_ref, t, nk):
    tiles = [tab_ref[idx_ref[t, k]] for k in range(nk)]
    return pltpu.bitcast(jnp.concatenate(tiles, axis=0), BF16)


def _diag_mask(n):
    lane = lax.broadcasted_iota(jnp.int32, (SUBLANES, n), 1)
    return (lane & (SUBLANES - 1)) == lax.broadcasted_iota(jnp.int32, (SUBLANES, n), 0)


def _peer_u_kernel(idx_ref, x_ref, u_ref, o_ref):
    tt, nk = idx_ref.shape
    diag = _diag_mask(SUBLANES * nk)

    def tok(t, carry):
        r = _gather_rows(u_ref, idx_ref, t, nk)
        xt = x_ref[t].astype(BF16)
        p = lax.dot_general(xt, r, (((1,), (1,)), ((), ())), preferred_element_type=F32)
        o_ref[pl.ds(t, 1), :] = jnp.sum(jnp.where(diag, p, 0.0), axis=0, keepdims=True)
        return carry

    lax.fori_loop(0, tt, tok, 0)


def _peer_u(eidx, x3, u_pk):
    m, nk = eidx.shape
    ne, sub4, ln = u_pk.shape
    tt = min(256, m)
    n = SUBLANES * nk
    return pl.pallas_call(
        _peer_u_kernel,
        grid=(m // tt,),
        in_specs=[pl.BlockSpec((tt, nk), lambda i: (i, 0), memory_space=pltpu.SMEM),
                  pl.BlockSpec((tt, SUBLANES, ln), lambda i: (i, 0, 0)),
                  pl.BlockSpec((ne, sub4, ln), lambda i: (0, 0, 0), pipeline_mode=pl.Buffered(1))],
        out_specs=pl.BlockSpec((tt, n), lambda i: (i, 0)),
        out_shape=jax.ShapeDtypeStruct((m, n), F32),
        compiler_params=_cp(("arbitrary",)),
        name="peer_u",
    )(eidx, x3, u_pk)


def _coef_kernel(part_ref, gate_ref, o_ref):
    n = part_ref.shape[1]
    nk = gate_ref.shape[1]
    sub = n // nk
    fold = (lax.broadcasted_iota(jnp.int32, (n, nk), 0) // sub
            == lax.broadcasted_iota(jnp.int32, (n, nk), 1)).astype(F32)
    spread = (lax.broadcasted_iota(jnp.int32, (nk, n), 1) // sub
              == lax.broadcasted_iota(jnp.int32, (nk, n), 0)).astype(F32)
    a = jnp.dot(part_ref[...], fold, precision=HI, preferred_element_type=F32)
    gelu = 0.5 * a * (1.0 + lax.erf(a * (1.0 / math.sqrt(2.0))))
    coef = gate_ref[...] * gelu
    o_ref[...] = jnp.dot(coef, spread, precision=HI, preferred_element_type=F32)


def _coef(part, gate):
    m, n = part.shape
    nk = gate.shape[1]
    tt = min(1024, m)
    return pl.pallas_call(
        _coef_kernel,
        grid=(m // tt,),
        in_specs=[pl.BlockSpec((tt, n), lambda i: (i, 0)),
                  pl.BlockSpec((tt, nk), lambda i: (i, 0))],
        out_specs=pl.BlockSpec((tt, n), lambda i: (i, 0)),
        out_shape=jax.ShapeDtypeStruct((m, n), F32),
        compiler_params=_cp(("parallel",)),
        name="peer_coef",
    )(part, gate)


def _peer_v_kernel(idx_ref, c_ref, v_ref, o_ref):
    tt, nk = idx_ref.shape
    diag = _diag_mask(SUBLANES * nk)

    def tok(t, carry):
        r = _gather_rows(v_ref, idx_ref, t, nk)
        c8 = jnp.where(diag, c_ref[pl.ds(t, 1), :], 0.0).astype(BF16)
        o_ref[t] = jnp.dot(c8, r, preferred_element_type=F32)
        return carry

    lax.fori_loop(0, tt, tok, 0)


def _peer_v(eidx, coef8, v_pk):
    m, nk = eidx.shape
    ne, sub4, ln = v_pk.shape
    tt = min(256, m)
    n = SUBLANES * nk
    return pl.pallas_call(
        _peer_v_kernel,
        grid=(m // tt,),
        in_specs=[pl.BlockSpec((tt, nk), lambda i: (i, 0), memory_space=pltpu.SMEM),
                  pl.BlockSpec((tt, n), lambda i: (i, 0)),
                  pl.BlockSpec((ne, sub4, ln), lambda i: (0, 0, 0), pipeline_mode=pl.Buffered(1))],
        out_specs=pl.BlockSpec((tt, SUBLANES, ln), lambda i: (i, 0, 0)),
        out_shape=jax.ShapeDtypeStruct((m, SUBLANES, ln), F32),
        compiler_params=_cp(("arbitrary",)),
        name="peer_v",
    )(eidx, coef8, v_pk)


def _res_ln_kernel(x_ref, y_ref, g_ref, b_ref, o_ref, *, alpha):
    o_ref[...] = _layer_norm(alpha * x_ref[...] + y_ref[...], g_ref[...], b_ref[...])


def _res_ln(x2, y2, g, b, alpha):
    m, d = x2.shape
    tm = min(1024, m)
    row = pl.BlockSpec((tm, d), lambda i: (i, 0))
    const = pl.BlockSpec((1, d), lambda i: (0, 0))
    return pl.pallas_call(
        functools.partial(_res_ln_kernel, alpha=alpha),
        grid=(m // tm,),
        in_specs=[row, row, const, const],
        out_specs=row,
        out_shape=jax.ShapeDtypeStruct((m, d), F32),
        compiler_params=_cp(("parallel",)),
        name="residual_ln",
    )(x2, y2, g, b)


def _peer(x2, wq_bf, keys_bf, u_pk, v_pk, heads, nkeys):
    m, d = x2.shape
    assert d == SUBLANES * LANES, "the expert stages keep one token row per (8, 128) vreg"
    eidx, gate = _route(x2, wq_bf, keys_bf, heads, nkeys)
    part = _peer_u(eidx, x2.reshape(m, SUBLANES, LANES), u_pk)
    coef8 = _coef(part, gate)
    return _peer_v(eidx, coef8, v_pk).reshape(m, d)


def _encoder_layer(x, lw, alpha):
    b, l, d = x.shape
    cw = lw["a_conv_w"].shape[1]
    x2 = x.reshape(b * l, d)
    p = _in_proj(x2, lw["w_in"], lw["b_in"])
    ya, x0, z = _mix(p.reshape(b, l, -1), lw["a_conv_w"], lw["h_conv_w"], lw["h_conv_b"], cw)
    yh = _hyena_mix(z, x0, lw["hy_bias"], lw["hf"])
    x1 = _out_proj(ya.reshape(b * l, cw), yh.reshape(b * l, -1), x2, lw["w_out_a"], lw["w_out_h"],
                   lw["ln1_g"], lw["ln1_b"], alpha)
    y = _peer(x1, lw["peer_wq"], lw["peer_keys"], lw["peer_u"], lw["peer_v"], lw["heads"], lw["nkeys"])
    return _res_ln(x1, y, lw["ln2_g"], lw["ln2_b"], alpha).reshape(b, l, d)


def kernel(x_prompt, x_sample, w_in, b_in, a_conv_w, h_conv_w, h_conv_b, hf_w1, hf_b1, hf_freq, hf_w2,
           hf_b2, hf_w3, hy_decay, hy_bias, w_out, ln1_g, ln1_b, peer_wq, peer_keys, peer_u, peer_v,
           ln2_g, ln2_b):
    depth = w_in.shape[0]
    alpha = (2.0 * depth) ** 0.25
    layers = []
    for i in range(depth):
        cw = a_conv_w.shape[2]
        heads, _, nkeys, half = peer_keys.shape[1:]
        layers.append(dict(
            w_in=w_in[i].astype(BF16), b_in=b_in[i][None].astype(F32),
            a_conv_w=a_conv_w[i], h_conv_w=h_conv_w[i], h_conv_b=h_conv_b[i][None],
            hf=(hf_w1[i], hf_b1[i], hf_freq[i], hf_w2[i], hf_b2[i], hf_w3[i], hy_decay[i]),
            hy_bias=hy_bias[i],
            w_out_a=w_out[i, :cw].astype(BF16), w_out_h=w_out[i, cw:].astype(BF16),
            ln1_g=ln1_g[i][None], ln1_b=ln1_b[i][None],
            peer_wq=peer_wq[i].astype(BF16),
            peer_keys=peer_keys[i].reshape(heads * 2, nkeys, half).astype(BF16),
            peer_u=_pack_table(peer_u[i]), peer_v=_pack_table(peer_v[i]),
            ln2_g=ln2_g[i][None], ln2_b=ln2_b[i][None], heads=heads, nkeys=nkeys))

    def trunk(x):
        for lw in layers:
            x = _encoder_layer(x, lw, alpha)
        return x

    return trunk(x_prompt), trunk(x_sample)
```

```python
import functools
import math

import jax
import jax.numpy as jnp
from jax import lax
from jax.experimental import pallas as pl
from jax.experimental.pallas import tpu as pltpu

F32 = jnp.float32
BF16 = jnp.bfloat16
HI = lax.Precision.HIGHEST

LANES = 128
SHORT_K = 3
PEER_TOPK = 16
LN_EPS = 1e-5
DFT_N2 = 128
VMEM_LIMIT = 56 * 1024 * 1024


def _cp(sem, vmem=VMEM_LIMIT):
    return pltpu.CompilerParams(dimension_semantics=sem, vmem_limit_bytes=vmem)


def _in_proj_kernel(x_ref, w_ref, b_ref, o_ref):
    o_ref[...] = jnp.dot(x_ref[...].astype(BF16), w_ref[...],
                         preferred_element_type=F32) + b_ref[...]


def _in_proj(x2, w_bf, b):
    m, d = x2.shape
    n = w_bf.shape[1]
    tm = min(512, m)
    return pl.pallas_call(
        _in_proj_kernel,
        grid=(m // tm,),
        in_specs=[pl.BlockSpec((tm, d), lambda i: (i, 0)),
                  pl.BlockSpec((d, n), lambda i: (0, 0)),
                  pl.BlockSpec((1, n), lambda i: (0, 0))],
        out_specs=pl.BlockSpec((tm, n), lambda i: (i, 0)),
        out_shape=jax.ShapeDtypeStruct((m, n), F32),
        compiler_params=_cp(("parallel",)),
        name="in_proj",
    )(x2, w_bf, b)


def _mix_kernel(pm_ref, pp_ref, pn_ref, aw_ref, hw_ref, hb_ref, ya_ref, x0_ref, z_ref, *, cw):
    i = pl.program_id(1)
    n = pl.num_programs(1)
    tl = pm_ref.shape[1]
    rows = lax.broadcasted_iota(jnp.int32, (tl, 1), 0)
    has_prev = i > 0
    has_next = i < n - 1

    def conv3(cur, prev_row, next_row, w):
        prev_row = jnp.where(has_prev, prev_row, 0.0)
        next_row = jnp.where(has_next, next_row, 0.0)
        dn = jnp.where(rows == 0, prev_row, pltpu.roll(cur, 1, 0))
        up = jnp.where(rows == tl - 1, next_row, pltpu.roll(cur, tl - 1, 0))
        return dn * w[0:1] + cur * w[1:2] + up * w[2:3]

    def sec(ref, r0, r1, j):
        return ref[0, r0:r1, j * cw:(j + 1) * cw]

    g = sec(pm_ref, 0, tl, 1) * sec(pm_ref, 0, tl, 2)
    gp = sec(pp_ref, 7, 8, 1) * sec(pp_ref, 7, 8, 2)
    gn = sec(pn_ref, 0, 1, 1) * sec(pn_ref, 0, 1, 2)
    ya_ref[0] = sec(pm_ref, 0, tl, 0) * conv3(g, gp, gn, aw_ref[...])

    def hconv(j):
        c = 3 + j
        w = hw_ref[:, j * cw:(j + 1) * cw]
        return conv3(sec(pm_ref, 0, tl, c), sec(pp_ref, 7, 8, c), sec(pn_ref, 0, 1, c), w) \
            + hb_ref[:, j * cw:(j + 1) * cw]

    x0_ref[0] = hconv(0)
    z_ref[0] = hconv(2) * hconv(1)


def _mix(p3, a_conv_w, h_conv_w, h_conv_b, cw):
    b, l, pw = p3.shape
    tl = min(512, l)
    nl = l // tl
    r8 = tl // 8
    out = jax.ShapeDtypeStruct((b, l, cw), F32)
    ospec = pl.BlockSpec((1, tl, cw), lambda bi, i: (bi, i, 0))
    return pl.pallas_call(
        functools.partial(_mix_kernel, cw=cw),
        grid=(b, nl),
        in_specs=[pl.BlockSpec((1, tl, pw), lambda bi, i: (bi, i, 0)),
                  pl.BlockSpec((1, 8, pw), lambda bi, i: (bi, jnp.maximum(i * r8 - 1, 0), 0)),
                  pl.BlockSpec((1, 8, pw), lambda bi, i: (bi, jnp.minimum((i + 1) * r8, l // 8 - 1), 0)),
                  pl.BlockSpec((SHORT_K, cw), lambda bi, i: (0, 0)),
                  pl.BlockSpec((SHORT_K, 3 * cw), lambda bi, i: (0, 0)),
                  pl.BlockSpec((1, 3 * cw), lambda bi, i: (0, 0))],
        out_specs=[ospec, ospec, ospec],
        out_shape=[out, out, out],
        compiler_params=_cp(("parallel", "parallel")),
        name="mix",
    )(p3, p3, p3, a_conv_w, h_conv_w, h_conv_b)


def _filter_kernel(t_ref, w_ref, f_ref, w1t_ref, w1c_ref, w1s_ref, b1_ref, fr_ref, w2_ref, b2_ref,
                   w3_ref, dec_ref, hk_ref, asum_ref, *, hw):
    i = pl.program_id(0)
    tl = t_ref.shape[0]
    t = t_ref[...]
    ang = w_ref[...] * f_ref[...]
    z1 = (t * w1t_ref[...]
          + jnp.dot(jnp.cos(ang), w1c_ref[...], precision=HI, preferred_element_type=F32)
          + jnp.dot(-jnp.sin(ang), w1s_ref[...], precision=HI, preferred_element_type=F32)
          + b1_ref[...])
    fr = fr_ref[...]
    h = jnp.sin(fr * z1)
    h = jnp.sin(fr * (jnp.dot(h, w2_ref[...], precision=HI, preferred_element_type=F32) + b2_ref[...]))
    h = jnp.dot(h, w3_ref[...], precision=HI, preferred_element_type=F32)
    h = h * jnp.exp(-t * jnp.abs(dec_ref[...]))
    rows = i * tl + lax.broadcasted_iota(jnp.int32, (tl, 1), 0)
    fwd = h[:, :hw]
    bwd = jnp.where(rows == 0, 0.0, h[:, hw:])
    hk_ref[0] = fwd
    hk_ref[1] = bwd
    s = jnp.concatenate([jnp.sum(jnp.abs(fwd), axis=0, keepdims=True),
                         jnp.sum(jnp.abs(bwd), axis=0, keepdims=True)], axis=0)

    @pl.when(i == 0)
    def _():
        asum_ref[...] = jnp.zeros_like(asum_ref)

    asum_ref[...] += s


def _pad2(a, r, c):
    return jnp.zeros((r, c), F32).at[:a.shape[0], :a.shape[1]].set(a.astype(F32))


def _hyena_filter(l, hf_w1, hf_b1, hf_freq, hf_w2, hf_b2, hf_w3, hy_decay):
    bands = (hf_w1.shape[0] - 1) // 2
    hid = hf_w1.shape[1]
    hw = hy_decay.shape[1]
    t = jnp.linspace(0.0, 1.0, l, dtype=F32)[:, None]
    w = (2.0 * math.pi) * jnp.arange(l, dtype=F32)[:, None] / l
    f = jnp.linspace(1e-4, bands - 1, bands, dtype=F32)[None, :]
    p = LANES
    tl = min(512, l)
    args = (t, w, _pad2(f, 1, p), _pad2(hf_w1[0:1], 1, p), _pad2(hf_w1[1:1 + bands], p, p),
            _pad2(hf_w1[1 + bands:], p, p), _pad2(hf_b1[None], 1, p), _pad2(hf_freq[None], 1, p),
            _pad2(hf_w2, p, p), _pad2(hf_b2[None], 1, p), _pad2(hf_w3, p, 2 * hw),
            hy_decay.reshape(1, 2 * hw).astype(F32))
    del hid
    const = lambda shape: pl.BlockSpec(shape, lambda i: (0, 0))
    return pl.pallas_call(
        functools.partial(_filter_kernel, hw=hw),
        grid=(l // tl,),
        in_specs=[pl.BlockSpec((tl, 1), lambda i: (i, 0)), pl.BlockSpec((tl, 1), lambda i: (i, 0)),
                  const((1, p)), const((1, p)), const((p, p)), const((p, p)), const((1, p)),
                  const((1, p)), const((p, p)), const((1, p)), const((p, 2 * hw)), const((1, 2 * hw))],
        out_specs=[pl.BlockSpec((2, tl, hw), lambda i: (0, i, 0)),
                   pl.BlockSpec((2, hw), lambda i: (0, 0))],
        out_shape=[jax.ShapeDtypeStruct((2, l, hw), F32), jax.ShapeDtypeStruct((2, hw), F32)],
        compiler_params=_cp(("arbitrary",)),
        name="hyena_filter",
    )(*args)


def _dft_consts(l):
    n2s = DFT_N2
    n = 2 * l
    n1s = n // n2s
    two_pi = 2.0 * math.pi
    k1 = jnp.arange(n1s, dtype=jnp.int32)[:, None]
    n1 = jnp.arange(n1s // 2, dtype=jnp.int32)[None, :]
    ang = -two_pi * ((k1 * n1) % n1s).astype(F32) / n1s
    fa = jnp.concatenate([jnp.cos(ang), jnp.sin(ang)], axis=0)
    k1 = jnp.arange(n1s, dtype=jnp.int32)[:, None, None]
    k2 = jnp.arange(n2s, dtype=jnp.int32)[None, :, None]
    n2 = jnp.arange(n2s, dtype=jnp.int32)[None, None, :]
    ang = -two_pi * ((n2 * k2 * n1s + n2 * k1) % n).astype(F32) / n
    mr, mi = jnp.cos(ang), jnp.sin(ang)
    g = jnp.concatenate([jnp.concatenate([mr, -mi], 2), jnp.concatenate([mi, mr], 2)], 1)
    mrt, mit = jnp.swapaxes(mr, 1, 2), jnp.swapaxes(mi, 1, 2)
    gi = jnp.concatenate([jnp.concatenate([mrt, mit], 2), jnp.concatenate([-mit, mrt], 2)], 1)
    n1 = jnp.arange(n1s // 2, dtype=jnp.int32)[:, None]
    k1 = jnp.arange(n1s, dtype=jnp.int32)[None, :]
    ang = two_pi * ((n1 * k1) % n1s).astype(F32) / n1s
    fb = jnp.concatenate([jnp.cos(ang), -jnp.sin(ang)], axis=1) / n
    return fa, g, gi, fb


def _lmat_kernel(f_ref, z_ref, o_ref):
    o_ref[0] = jnp.dot(f_ref[...], z_ref[0], precision=HI, preferred_element_type=F32)


def _dft_stage1(fa, z3):
    b, r, cols = z3.shape
    m = fa.shape[0]
    tn = min(4096, cols)
    return pl.pallas_call(
        _lmat_kernel,
        grid=(b, cols // tn),
        in_specs=[pl.BlockSpec((m, r), lambda bi, j: (0, 0)),
                  pl.BlockSpec((1, r, tn), lambda bi, j: (bi, 0, j))],
        out_specs=pl.BlockSpec((1, m, tn), lambda bi, j: (bi, 0, j)),
        out_shape=jax.ShapeDtypeStruct((b, m, cols), F32),
        compiler_params=_cp(("parallel", "parallel")),
        name="dft_stage1",
    )(fa, z3)


def _kf_kernel(a_ref, g_ref, asum_ref, kf_ref):
    n2 = a_ref.shape[3]
    c = a_ref.shape[4]
    g = g_ref[0]
    f0 = jnp.dot(g, a_ref[0, :, 0].reshape(2 * n2, c), precision=HI, preferred_element_type=F32)
    f1 = jnp.dot(g, a_ref[1, :, 0].reshape(2 * n2, c), precision=HI, preferred_element_type=F32)
    inv = 1.0 / (asum_ref[0:1] + asum_ref[1:2])
    kf_ref[0] = jnp.concatenate([(f0[:n2] + f1[:n2]) * inv, (f0[n2:] - f1[n2:]) * inv], axis=0)


def _filter_spectrum(a5, g, asum):
    _, _, n1s, n2, c = a5.shape
    return pl.pallas_call(
        _kf_kernel,
        grid=(n1s,),
        in_specs=[pl.BlockSpec((2, 2, 1, n2, c), lambda k: (0, 0, k, 0, 0)),
                  pl.BlockSpec((1, 2 * n2, 2 * n2), lambda k: (k, 0, 0)),
                  pl.BlockSpec((2, c), lambda k: (0, 0))],
        out_specs=pl.BlockSpec((1, 2 * n2, c), lambda k: (k, 0, 0)),
        out_shape=jax.ShapeDtypeStruct((n1s, 2 * n2, c), F32),
        compiler_params=_cp(("parallel",)),
        name="filter_spectrum",
    )(a5, g, asum)


def _spec_kernel(a_ref, g_ref, kf_ref, gi_ref, d_ref):
    n2 = a_ref.shape[3]
    c = a_ref.shape[4]
    a = a_ref[0, :, 0].reshape(2 * n2, c)
    xf = jnp.dot(g_ref[0], a, precision=HI, preferred_element_type=F32)
    kf = kf_ref[0]
    xr, xi, kr, ki = xf[:n2], xf[n2:], kf[:n2], kf[n2:]
    y = jnp.concatenate([xr * kr - xi * ki, xr * ki + xi * kr], axis=0)
    d = jnp.dot(gi_ref[0], y, precision=HI, preferred_element_type=F32)
    d_ref[0, :, 0] = d.reshape(2, n2, c)


def _spectral_multiply(a5, g, kf, gi):
    b, _, n1s, n2, c = a5.shape
    blk = pl.BlockSpec((1, 2, 1, n2, c), lambda k, bi: (bi, 0, k, 0, 0))
    mat = pl.BlockSpec((1, 2 * n2, 2 * n2), lambda k, bi: (k, 0, 0))
    return pl.pallas_call(
        _spec_kernel,
        grid=(n1s, b),
        in_specs=[blk, mat, pl.BlockSpec((1, 2 * n2, c), lambda k, bi: (k, 0, 0)), mat],
        out_specs=blk,
        out_shape=jax.ShapeDtypeStruct(a5.shape, F32),
        compiler_params=_cp(("parallel", "arbitrary")),
        name="spectral_multiply",
    )(a5, g, kf, gi)


def _inv_kernel(f_ref, d_ref, z_ref, x0_ref, bias_ref, o_ref):
    y = jnp.dot(f_ref[...], d_ref[0], precision=HI, preferred_element_type=F32)
    o_ref[0] = x0_ref[0] * (y + z_ref[0] * bias_ref[...])


def _dft_final(fb, d3, z3, x03, bias_t):
    b, m2, cols = d3.shape
    r = fb.shape[0]
    tn = min(4096, cols)
    rspec = pl.BlockSpec((1, r, tn), lambda bi, j: (bi, 0, j))
    return pl.pallas_call(
        _inv_kernel,
        grid=(b, cols // tn),
        in_specs=[pl.BlockSpec((r, m2), lambda bi, j: (0, 0)),
                  pl.BlockSpec((1, m2, tn), lambda bi, j: (bi, 0, j)),
                  rspec, rspec,
                  pl.BlockSpec((1, tn), lambda bi, j: (0, j))],
        out_specs=rspec,
        out_shape=jax.ShapeDtypeStruct((b, r, cols), F32),
        compiler_params=_cp(("parallel", "parallel")),
        name="dft_final",
    )(fb, d3, z3, x03, bias_t)


def _hyena_mix(z, x0, hy_bias, hf):
    b, l, c = z.shape
    n2 = DFT_N2
    n1s = 2 * l // n2
    fa, g, gi, fb = _dft_consts(l)
    hk, asum = _hyena_filter(l, *hf)
    cols = n2 * c
    ka = _dft_stage1(fa, hk.reshape(2, n1s // 2, cols))
    kf = _filter_spectrum(ka.reshape(2, 2, n1s, n2, c), g, asum)
    z3 = z.reshape(b, n1s // 2, cols)
    a = _dft_stage1(fa, z3)
    d = _spectral_multiply(a.reshape(b, 2, n1s, n2, c), g, kf, gi)
    bias_t = jnp.tile(hy_bias.astype(F32), n2)[None, :]
    yh = _dft_final(fb, d.reshape(b, 2 * n1s, cols), z3, x0.reshape(b, n1s // 2, cols), bias_t)
    return yh.reshape(b, l, c)


def _layer_norm(r, g, b):
    mu = jnp.mean(r, axis=-1, keepdims=True)
    rc = r - mu
    var = jnp.mean(rc * rc, axis=-1, keepdims=True)
    return rc * lax.rsqrt(var + LN_EPS) * g + b


def _out_proj_kernel(ya_ref, yh_ref, x_ref, wa_ref, wh_ref, g_ref, b_ref, o_ref, *, alpha):
    mix = (jnp.dot(ya_ref[...].astype(BF16), wa_ref[...], preferred_element_type=F32)
           + jnp.dot(yh_ref[...].astype(BF16), wh_ref[...], preferred_element_type=F32))
    o_ref[...] = _layer_norm(alpha * x_ref[...] + mix, g_ref[...], b_ref[...])


def _out_proj(ya, yh, x2, wa_bf, wh_bf, g, b, alpha):
    m, d = x2.shape
    cw = ya.shape[1]
    tm = min(512, m)
    row = lambda w: pl.BlockSpec((tm, w), lambda i: (i, 0))
    const = lambda shape: pl.BlockSpec(shape, lambda i: (0, 0))
    return pl.pallas_call(
        functools.partial(_out_proj_kernel, alpha=alpha),
        grid=(m // tm,),
        in_specs=[row(cw), row(cw), row(d), const((cw, d)), const((cw, d)), const((1, d)), const((1, d))],
        out_specs=row(d),
        out_shape=jax.ShapeDtypeStruct((m, d), F32),
        compiler_params=_cp(("parallel",)),
        name="out_proj_ln",
    )(ya, yh, x2, wa_bf, wh_bf, g, b)


def _topk_rows(s, k, payload=None):
    n = s.shape[0]
    iota = lax.broadcasted_iota(jnp.int32, s.shape, 0).astype(F32)
    vals, poss, pays = [], [], []
    for _ in range(k):
        m = jnp.max(s, axis=0, keepdims=True)
        pos = jnp.min(jnp.where(s == m, iota, float(n)), axis=0, keepdims=True)
        hit = iota == pos
        vals.append(m)
        poss.append(pos)
        if payload is not None:
            pays.append(jnp.max(jnp.where(hit, payload, -1.0), axis=0, keepdims=True))
        s = jnp.where(hit, -jnp.inf, s)
    cat = lambda xs: jnp.concatenate(xs, axis=0)
    return cat(vals), cat(poss), (cat(pays) if payload is not None else None)


def _route_kernel(x_ref, wq_ref, keys_ref, eidx_ref, gate_ref, q_scr, e_scr, g_scr, *, heads, nkeys):
    kk = PEER_TOPK
    half = keys_ref.shape[2]
    q_scr[...] = jnp.dot(x_ref[...].astype(BF16), wq_ref[...], preferred_element_type=F32)

    def head(h, carry):
        tops = []
        for p in range(2):
            c = h * 2 + p
            qc = q_scr[:, pl.ds(pl.multiple_of(c * half, half), half)].astype(BF16)
            st = lax.dot_general(keys_ref[c], qc, (((1,), (1,)), ((), ())),
                                 preferred_element_type=F32)
            v, i, _ = _topk_rows(st, kk)
            tops.append((v, i))
        (v1, i1), (v2, i2) = tops
        cand = jnp.concatenate([v1[a:a + 1] + v2 for a in range(kk)], axis=0)
        cidx = jnp.concatenate([i1[a:a + 1] * float(nkeys) + i2 for a in range(kk)], axis=0)
        best, _, e = _topk_rows(cand, kk, payload=cidx)
        ex = jnp.exp(best - jnp.max(best, axis=0, keepdims=True))
        gate = ex / jnp.sum(ex, axis=0, keepdims=True)
        r0 = pl.multiple_of(h * kk, kk)
        e_scr[pl.ds(r0, kk), :] = e
        g_scr[pl.ds(r0, kk), :] = gate
        return carry

    lax.fori_loop(0, heads, head, 0)
    eidx_ref[...] = e_scr[...].T.astype(jnp.int32)
    gate_ref[...] = g_scr[...].T


def _route(x2, wq_bf, keys_bf, heads, nkeys):
    m, d = x2.shape
    qd = wq_bf.shape[1]
    tt = min(256, m)
    hk = heads * PEER_TOPK
    return pl.pallas_call(
        functools.partial(_route_kernel, heads=heads, nkeys=nkeys),
        grid=(m // tt,),
        in_specs=[pl.BlockSpec((tt, d), lambda i: (i, 0)),
                  pl.BlockSpec((d, qd), lambda i: (0, 0)),
                  pl.BlockSpec(keys_bf.shape, lambda i: (0, 0, 0))],
        out_specs=[pl.BlockSpec((tt, hk), lambda i: (i, 0)),
                   pl.BlockSpec((tt, hk), lambda i: (i, 0))],
        out_shape=[jax.ShapeDtypeStruct((m, hk), jnp.int32), jax.ShapeDtypeStruct((m, hk), F32)],
        scratch_shapes=[pltpu.VMEM((tt, qd), F32), pltpu.VMEM((hk, tt), F32), pltpu.VMEM((hk, tt), F32)],
        compiler_params=_cp(("parallel",)),
        name="peer_route",
    )(x2, wq_bf, keys_bf)


SUBLANES = 8
TOKEN_UNROLL = 8


def _pack_table(tab):
    e, d = tab.shape
    bits = lax.bitcast_convert_type(tab.astype(BF16).reshape(e, d // (2 * LANES), 2, LANES), jnp.uint16)
    bits = bits.astype(jnp.uint32)
    return lax.bitcast_convert_type(bits[:, :, 0, :] | (bits[:, :, 1, :] << 16), jnp.int32)


def _gather_rows(tab_ref, idx_ref, t, nk):
    tiles = [tab_ref[idx_ref[t, k]] for k in range(nk)]
    return pltpu.bitcast(jnp.concatenate(tiles, axis=0), BF16)


def _diag_mask(n):
    lane = lax.broadcasted_iota(jnp.int32, (SUBLANES, n), 1)
    return (lane & (SUBLANES - 1)) == lax.broadcasted_iota(jnp.int32, (SUBLANES, n), 0)


def _peer_u_kernel(idx_ref, x_ref, u_ref, o_ref):
    tt, nk = idx_ref.shape
    diag = _diag_mask(SUBLANES * nk)

    def toks(i, carry):
        for j in range(TOKEN_UNROLL):
            t = i * TOKEN_UNROLL + j
            r = _gather_rows(u_ref, idx_ref, t, nk)
            xt = x_ref[t].astype(BF16)
            p = lax.dot_general(xt, r, (((1,), (1,)), ((), ())), preferred_element_type=F32)
            o_ref[pl.ds(t, 1), :] = jnp.sum(jnp.where(diag, p, 0.0), axis=0, keepdims=True)
        return carry

    lax.fori_loop(0, tt // TOKEN_UNROLL, toks, 0)


def _peer_u(eidx, x3, u_pk):
    m, nk = eidx.shape
    ne, sub4, ln = u_pk.shape
    tt = min(256, m)
    n = SUBLANES * nk
    return pl.pallas_call(
        _peer_u_kernel,
        grid=(m // tt,),
        in_specs=[pl.BlockSpec((tt, nk), lambda i: (i, 0), memory_space=pltpu.SMEM),
                  pl.BlockSpec((tt, SUBLANES, ln), lambda i: (i, 0, 0)),
                  pl.BlockSpec((ne, sub4, ln), lambda i: (0, 0, 0), pipeline_mode=pl.Buffered(1))],
        out_specs=pl.BlockSpec((tt, n), lambda i: (i, 0)),
        out_shape=jax.ShapeDtypeStruct((m, n), F32),
        compiler_params=_cp(("arbitrary",)),
        name="peer_u",
    )(eidx, x3, u_pk)


def _coef_kernel(part_ref, gate_ref, o_ref):
    n = part_ref.shape[1]
    nk = gate_ref.shape[1]
    sub = n // nk
    fold = (lax.broadcasted_iota(jnp.int32, (n, nk), 0) // sub
            == lax.broadcasted_iota(jnp.int32, (n, nk), 1)).astype(F32)
    spread = (lax.broadcasted_iota(jnp.int32, (nk, n), 1) // sub
              == lax.broadcasted_iota(jnp.int32, (nk, n), 0)).astype(F32)
    a = jnp.dot(part_ref[...], fold, precision=HI, preferred_element_type=F32)
    gelu = 0.5 * a * (1.0 + lax.erf(a * (1.0 / math.sqrt(2.0))))
    coef = gate_ref[...] * gelu
    o_ref[...] = jnp.dot(coef, spread, precision=HI, preferred_element_type=F32)


def _coef(part, gate):
    m, n = part.shape
    nk = gate.shape[1]
    tt = min(1024, m)
    return pl.pallas_call(
        _coef_kernel,
        grid=(m // tt,),
        in_specs=[pl.BlockSpec((tt, n), lambda i: (i, 0)),
                  pl.BlockSpec((tt, nk), lambda i: (i, 0))],
        out_specs=pl.BlockSpec((tt, n), lambda i: (i, 0)),
        out_shape=jax.ShapeDtypeStruct((m, n), F32),
        compiler_params=_cp(("parallel",)),
        name="peer_coef",
    )(part, gate)


def _peer_v_kernel(idx_ref, c_ref, v_ref, o_ref):
    tt, nk = idx_ref.shape
    diag = _diag_mask(SUBLANES * nk)

    def toks(i, carry):
        for j in range(TOKEN_UNROLL):
            t = i * TOKEN_UNROLL + j
            r = _gather_rows(v_ref, idx_ref, t, nk)
            c8 = jnp.where(diag, c_ref[pl.ds(t, 1), :], 0.0).astype(BF16)
            o_ref[t] = jnp.dot(c8, r, preferred_element_type=F32)
        return carry

    lax.fori_loop(0, tt // TOKEN_UNROLL, toks, 0)


def _peer_v(eidx, coef8, v_pk):
    m, nk = eidx.shape
    ne, sub4, ln = v_pk.shape
    tt = min(256, m)
    n = SUBLANES * nk
    return pl.pallas_call(
        _peer_v_kernel,
        grid=(m // tt,),
        in_specs=[pl.BlockSpec((tt, nk), lambda i: (i, 0), memory_space=pltpu.SMEM),
                  pl.BlockSpec((tt, n), lambda i: (i, 0)),
                  pl.BlockSpec((ne, sub4, ln), lambda i: (0, 0, 0), pipeline_mode=pl.Buffered(1))],
        out_specs=pl.BlockSpec((tt, SUBLANES, ln), lambda i: (i, 0, 0)),
        out_shape=jax.ShapeDtypeStruct((m, SUBLANES, ln), F32),
        compiler_params=_cp(("arbitrary",)),
        name="peer_v",
    )(eidx, coef8, v_pk)


def _res_ln_kernel(x_ref, y_ref, g_ref, b_ref, o_ref, *, alpha):
    o_ref[...] = _layer_norm(alpha * x_ref[...] + y_ref[...], g_ref[...], b_ref[...])


def _res_ln(x2, y2, g, b, alpha):
    m, d = x2.shape
    tm = min(1024, m)
    row = pl.BlockSpec((tm, d), lambda i: (i, 0))
    const = pl.BlockSpec((1, d), lambda i: (0, 0))
    return pl.pallas_call(
        functools.partial(_res_ln_kernel, alpha=alpha),
        grid=(m // tm,),
        in_specs=[row, row, const, const],
        out_specs=row,
        out_shape=jax.ShapeDtypeStruct((m, d), F32),
        compiler_params=_cp(("parallel",)),
        name="residual_ln",
    )(x2, y2, g, b)


def _peer(x2, wq_bf, keys_bf, u_pk, v_pk, heads, nkeys):
    m, d = x2.shape
    assert d == SUBLANES * LANES, "the expert stages keep one token row per (8, 128) vreg"
    eidx, gate = _route(x2, wq_bf, keys_bf, heads, nkeys)
    part = _peer_u(eidx, x2.reshape(m, SUBLANES, LANES), u_pk)
    coef8 = _coef(part, gate)
    return _peer_v(eidx, coef8, v_pk).reshape(m, d)


def _encoder_layer(x, lw, alpha):
    b, l, d = x.shape
    cw = lw["a_conv_w"].shape[1]
    x2 = x.reshape(b * l, d)
    p = _in_proj(x2, lw["w_in"], lw["b_in"])
    ya, x0, z = _mix(p.reshape(b, l, -1), lw["a_conv_w"], lw["h_conv_w"], lw["h_conv_b"], cw)
    yh = _hyena_mix(z, x0, lw["hy_bias"], lw["hf"])
    x1 = _out_proj(ya.reshape(b * l, cw), yh.reshape(b * l, -1), x2, lw["w_out_a"], lw["w_out_h"],
                   lw["ln1_g"], lw["ln1_b"], alpha)
    y = _peer(x1, lw["peer_wq"], lw["peer_keys"], lw["peer_u"], lw["peer_v"], lw["heads"], lw["nkeys"])
    return _res_ln(x1, y, lw["ln2_g"], lw["ln2_b"], alpha).reshape(b, l, d)


def kernel(x_prompt, x_sample, w_in, b_in, a_conv_w, h_conv_w, h_conv_b, hf_w1, hf_b1, hf_freq, hf_w2,
           hf_b2, hf_w3, hy_decay, hy_bias, w_out, ln1_g, ln1_b, peer_wq, peer_keys, peer_u, peer_v,
           ln2_g, ln2_b):
    depth = w_in.shape[0]
    alpha = (2.0 * depth) ** 0.25
    layers = []
    for i in range(depth):
        cw = a_conv_w.shape[2]
        heads, _, nkeys, half = peer_keys.shape[1:]
        layers.append(dict(
            w_in=w_in[i].astype(BF16), b_in=b_in[i][None].astype(F32),
            a_conv_w=a_conv_w[i], h_conv_w=h_conv_w[i], h_conv_b=h_conv_b[i][None],
            hf=(hf_w1[i], hf_b1[i], hf_freq[i], hf_w2[i], hf_b2[i], hf_w3[i], hy_decay[i]),
            hy_bias=hy_bias[i],
            w_out_a=w_out[i, :cw].astype(BF16), w_out_h=w_out[i, cw:].astype(BF16),
            ln1_g=ln1_g[i][None], ln1_b=ln1_b[i][None],
            peer_wq=peer_wq[i].astype(BF16),
            peer_keys=peer_keys[i].reshape(heads * 2, nkeys, half).astype(BF16),
            peer_u=_pack_table(peer_u[i]), peer_v=_pack_table(peer_v[i]),
            ln2_g=ln2_g[i][None], ln2_b=ln2_b[i][None], heads=heads, nkeys=nkeys))

    def trunk(x):
        for lw in layers:
            x = _encoder_layer(x, lw, alpha)
        return x

    return trunk(x_prompt), trunk(x_sample)
```

```python
import functools
import math

import jax
import jax.numpy as jnp
from jax import lax
from jax.experimental import pallas as pl
from jax.experimental.pallas import tpu as pltpu

F32 = jnp.float32
BF16 = jnp.bfloat16
HI = lax.Precision.HIGHEST

LANES = 128
SHORT_K = 3
PEER_TOPK = 16
LN_EPS = 1e-5
DFT_N2 = 128
VMEM_LIMIT = 56 * 1024 * 1024


def _cp(sem, vmem=VMEM_LIMIT):
    return pltpu.CompilerParams(dimension_semantics=sem, vmem_limit_bytes=vmem)


def _in_proj_kernel(x_ref, w_ref, b_ref, o_ref):
    o_ref[...] = jnp.dot(x_ref[...].astype(BF16), w_ref[...],
                         preferred_element_type=F32) + b_ref[...]


def _in_proj(x2, w_bf, b):
    m, d = x2.shape
    n = w_bf.shape[1]
    tm = min(512, m)
    return pl.pallas_call(
        _in_proj_kernel,
        grid=(m // tm,),
        in_specs=[pl.BlockSpec((tm, d), lambda i: (i, 0)),
                  pl.BlockSpec((d, n), lambda i: (0, 0)),
                  pl.BlockSpec((1, n), lambda i: (0, 0))],
        out_specs=pl.BlockSpec((tm, n), lambda i: (i, 0)),
        out_shape=jax.ShapeDtypeStruct((m, n), F32),
        compiler_params=_cp(("parallel",)),
        name="in_proj",
    )(x2, w_bf, b)


def _mix_kernel(pm_ref, pp_ref, pn_ref, aw_ref, hw_ref, hb_ref, ya_ref, x0_ref, z_ref, *, cw):
    i = pl.program_id(1)
    n = pl.num_programs(1)
    tl = pm_ref.shape[1]
    rows = lax.broadcasted_iota(jnp.int32, (tl, 1), 0)
    has_prev = i > 0
    has_next = i < n - 1

    def conv3(cur, prev_row, next_row, w):
        prev_row = jnp.where(has_prev, prev_row, 0.0)
        next_row = jnp.where(has_next, next_row, 0.0)
        dn = jnp.where(rows == 0, prev_row, pltpu.roll(cur, 1, 0))
        up = jnp.where(rows == tl - 1, next_row, pltpu.roll(cur, tl - 1, 0))
        return dn * w[0:1] + cur * w[1:2] + up * w[2:3]

    def sec(ref, r0, r1, j):
        return ref[0, r0:r1, j * cw:(j + 1) * cw]

    g = sec(pm_ref, 0, tl, 1) * sec(pm_ref, 0, tl, 2)
    gp = sec(pp_ref, 7, 8, 1) * sec(pp_ref, 7, 8, 2)
    gn = sec(pn_ref, 0, 1, 1) * sec(pn_ref, 0, 1, 2)
    ya_ref[0] = sec(pm_ref, 0, tl, 0) * conv3(g, gp, gn, aw_ref[...])

    def hconv(j):
        c = 3 + j
        w = hw_ref[:, j * cw:(j + 1) * cw]
        return conv3(sec(pm_ref, 0, tl, c), sec(pp_ref, 7, 8, c), sec(pn_ref, 0, 1, c), w) \
            + hb_ref[:, j * cw:(j + 1) * cw]

    x0_ref[0] = hconv(0)
    z_ref[0] = hconv(2) * hconv(1)


def _mix(p3, a_conv_w, h_conv_w, h_conv_b, cw):
    b, l, pw = p3.shape
    tl = min(512, l)
    nl = l // tl
    r8 = tl // 8
    out = jax.ShapeDtypeStruct((b, l, cw), F32)
    ospec = pl.BlockSpec((1, tl, cw), lambda bi, i: (bi, i, 0))
    return pl.pallas_call(
        functools.partial(_mix_kernel, cw=cw),
        grid=(b, nl),
        in_specs=[pl.BlockSpec((1, tl, pw), lambda bi, i: (bi, i, 0)),
                  pl.BlockSpec((1, 8, pw), lambda bi, i: (bi, jnp.maximum(i * r8 - 1, 0), 0)),
                  pl.BlockSpec((1, 8, pw), lambda bi, i: (bi, jnp.minimum((i + 1) * r8, l // 8 - 1), 0)),
                  pl.BlockSpec((SHORT_K, cw), lambda bi, i: (0, 0)),
                  pl.BlockSpec((SHORT_K, 3 * cw), lambda bi, i: (0, 0)),
                  pl.BlockSpec((1, 3 * cw), lambda bi, i: (0, 0))],
        out_specs=[ospec, ospec, ospec],
        out_shape=[out, out, out],
        compiler_params=_cp(("parallel", "parallel")),
        name="mix",
    )(p3, p3, p3, a_conv_w, h_conv_w, h_conv_b)


def _filter_kernel(t_ref, w_ref, f_ref, w1t_ref, w1c_ref, w1s_ref, b1_ref, fr_ref, w2_ref, b2_ref,
                   w3_ref, dec_ref, hk_ref, asum_ref, *, hw):
    i = pl.program_id(0)
    tl = t_ref.shape[0]
    t = t_ref[...]
    ang = w_ref[...] * f_ref[...]
    z1 = (t * w1t_ref[...]
          + jnp.dot(jnp.cos(ang), w1c_ref[...], precision=HI, preferred_element_type=F32)
          + jnp.dot(-jnp.sin(ang), w1s_ref[...], precision=HI, preferred_element_type=F32)
          + b1_ref[...])
    fr = fr_ref[...]
    h = jnp.sin(fr * z1)
    h = jnp.sin(fr * (jnp.dot(h, w2_ref[...], precision=HI, preferred_element_type=F32) + b2_ref[...]))
    h = jnp.dot(h, w3_ref[...], precision=HI, preferred_element_type=F32)
    h = h * jnp.exp(-t * jnp.abs(dec_ref[...]))
    rows = i * tl + lax.broadcasted_iota(jnp.int32, (tl, 1), 0)
    fwd = h[:, :hw]
    bwd = jnp.where(rows == 0, 0.0, h[:, hw:])
    hk_ref[0] = fwd
    hk_ref[1] = bwd
    s = jnp.concatenate([jnp.sum(jnp.abs(fwd), axis=0, keepdims=True),
                         jnp.sum(jnp.abs(bwd), axis=0, keepdims=True)], axis=0)

    @pl.when(i == 0)
    def _():
        asum_ref[...] = jnp.zeros_like(asum_ref)

    asum_ref[...] += s


def _pad2(a, r, c):
    return jnp.zeros((r, c), F32).at[:a.shape[0], :a.shape[1]].set(a.astype(F32))


def _hyena_filter(l, hf_w1, hf_b1, hf_freq, hf_w2, hf_b2, hf_w3, hy_decay):
    bands = (hf_w1.shape[0] - 1) // 2
    hid = hf_w1.shape[1]
    hw = hy_decay.shape[1]
    t = jnp.linspace(0.0, 1.0, l, dtype=F32)[:, None]
    w = (2.0 * math.pi) * jnp.arange(l, dtype=F32)[:, None] / l
    f = jnp.linspace(1e-4, bands - 1, bands, dtype=F32)[None, :]
    p = LANES
    tl = min(512, l)
    args = (t, w, _pad2(f, 1, p), _pad2(hf_w1[0:1], 1, p), _pad2(hf_w1[1:1 + bands], p, p),
            _pad2(hf_w1[1 + bands:], p, p), _pad2(hf_b1[None], 1, p), _pad2(hf_freq[None], 1, p),
            _pad2(hf_w2, p, p), _pad2(hf_b2[None], 1, p), _pad2(hf_w3, p, 2 * hw),
            hy_decay.reshape(1, 2 * hw).astype(F32))
    del hid
    const = lambda shape: pl.BlockSpec(shape, lambda i: (0, 0))
    return pl.pallas_call(
        functools.partial(_filter_kernel, hw=hw),
        grid=(l // tl,),
        in_specs=[pl.BlockSpec((tl, 1), lambda i: (i, 0)), pl.BlockSpec((tl, 1), lambda i: (i, 0)),
                  const((1, p)), const((1, p)), const((p, p)), const((p, p)), const((1, p)),
                  const((1, p)), const((p, p)), const((1, p)), const((p, 2 * hw)), const((1, 2 * hw))],
        out_specs=[pl.BlockSpec((2, tl, hw), lambda i: (0, i, 0)),
                   pl.BlockSpec((2, hw), lambda i: (0, 0))],
        out_shape=[jax.ShapeDtypeStruct((2, l, hw), F32), jax.ShapeDtypeStruct((2, hw), F32)],
        compiler_params=_cp(("arbitrary",)),
        name="hyena_filter",
    )(*args)


def _dft_consts(l):
    n2s = DFT_N2
    n = 2 * l
    n1s = n // n2s
    two_pi = 2.0 * math.pi
    k1 = jnp.arange(n1s, dtype=jnp.int32)[:, None]
    n1 = jnp.arange(n1s // 2, dtype=jnp.int32)[None, :]
    ang = -two_pi * ((k1 * n1) % n1s).astype(F32) / n1s
    fa = jnp.concatenate([jnp.cos(ang), jnp.sin(ang)], axis=0)
    k1 = jnp.arange(n1s, dtype=jnp.int32)[:, None, None]
    k2 = jnp.arange(n2s, dtype=jnp.int32)[None, :, None]
    n2 = jnp.arange(n2s, dtype=jnp.int32)[None, None, :]
    ang = -two_pi * ((n2 * k2 * n1s + n2 * k1) % n).astype(F32) / n
    mr, mi = jnp.cos(ang), jnp.sin(ang)
    g = jnp.concatenate([jnp.concatenate([mr, -mi], 2), jnp.concatenate([mi, mr], 2)], 1)
    mrt, mit = jnp.swapaxes(mr, 1, 2), jnp.swapaxes(mi, 1, 2)
    gi = jnp.concatenate([jnp.concatenate([mrt, mit], 2), jnp.concatenate([-mit, mrt], 2)], 1)
    n1 = jnp.arange(n1s // 2, dtype=jnp.int32)[:, None]
    k1 = jnp.arange(n1s, dtype=jnp.int32)[None, :]
    ang = two_pi * ((n1 * k1) % n1s).astype(F32) / n1s
    fb = jnp.concatenate([jnp.cos(ang), -jnp.sin(ang)], axis=1) / n
    return fa, g, gi, fb


def _lmat_kernel(f_ref, z_ref, o_ref):
    o_ref[0] = jnp.dot(f_ref[...], z_ref[0], precision=HI, preferred_element_type=F32)


def _dft_stage1(fa, z3):
    b, r, cols = z3.shape
    m = fa.shape[0]
    tn = min(4096, cols)
    return pl.pallas_call(
        _lmat_kernel,
        grid=(b, cols // tn),
        in_specs=[pl.BlockSpec((m, r), lambda bi, j: (0, 0)),
                  pl.BlockSpec((1, r, tn), lambda bi, j: (bi, 0, j))],
        out_specs=pl.BlockSpec((1, m, tn), lambda bi, j: (bi, 0, j)),
        out_shape=jax.ShapeDtypeStruct((b, m, cols), F32),
        compiler_params=_cp(("parallel", "parallel")),
        name="dft_stage1",
    )(fa, z3)


def _kf_kernel(a_ref, g_ref, asum_ref, kf_ref):
    n2 = a_ref.shape[3]
    c = a_ref.shape[4]
    g = g_ref[0]
    f0 = jnp.dot(g, a_ref[0, :, 0].reshape(2 * n2, c), precision=HI, preferred_element_type=F32)
    f1 = jnp.dot(g, a_ref[1, :, 0].reshape(2 * n2, c), precision=HI, preferred_element_type=F32)
    inv = 1.0 / (asum_ref[0:1] + asum_ref[1:2])
    kf_ref[0] = jnp.concatenate([(f0[:n2] + f1[:n2]) * inv, (f0[n2:] - f1[n2:]) * inv], axis=0)


def _filter_spectrum(a5, g, asum):
    _, _, n1s, n2, c = a5.shape
    return pl.pallas_call(
        _kf_kernel,
        grid=(n1s,),
        in_specs=[pl.BlockSpec((2, 2, 1, n2, c), lambda k: (0, 0, k, 0, 0)),
                  pl.BlockSpec((1, 2 * n2, 2 * n2), lambda k: (k, 0, 0)),
                  pl.BlockSpec((2, c), lambda k: (0, 0))],
        out_specs=pl.BlockSpec((1, 2 * n2, c), lambda k: (k, 0, 0)),
        out_shape=jax.ShapeDtypeStruct((n1s, 2 * n2, c), F32),
        compiler_params=_cp(("parallel",)),
        name="filter_spectrum",
    )(a5, g, asum)


def _spec_kernel(a_ref, g_ref, kf_ref, gi_ref, d_ref):
    n2 = a_ref.shape[3]
    c = a_ref.shape[4]
    a = a_ref[0, :, 0].reshape(2 * n2, c)
    xf = jnp.dot(g_ref[0], a, precision=HI, preferred_element_type=F32)
    kf = kf_ref[0]
    xr, xi, kr, ki = xf[:n2], xf[n2:], kf[:n2], kf[n2:]
    y = jnp.concatenate([xr * kr - xi * ki, xr * ki + xi * kr], axis=0)
    d = jnp.dot(gi_ref[0], y, precision=HI, preferred_element_type=F32)
    d_ref[0, :, 0] = d.reshape(2, n2, c)


def _spectral_multiply(a5, g, kf, gi):
    b, _, n1s, n2, c = a5.shape
    blk = pl.BlockSpec((1, 2, 1, n2, c), lambda k, bi: (bi, 0, k, 0, 0))
    mat = pl.BlockSpec((1, 2 * n2, 2 * n2), lambda k, bi: (k, 0, 0))
    return pl.pallas_call(
        _spec_kernel,
        grid=(n1s, b),
        in_specs=[blk, mat, pl.BlockSpec((1, 2 * n2, c), lambda k, bi: (k, 0, 0)), mat],
        out_specs=blk,
        out_shape=jax.ShapeDtypeStruct(a5.shape, F32),
        compiler_params=_cp(("parallel", "arbitrary")),
        name="spectral_multiply",
    )(a5, g, kf, gi)


def _inv_kernel(f_ref, d_ref, z_ref, x0_ref, bias_ref, o_ref):
    y = jnp.dot(f_ref[...], d_ref[0], precision=HI, preferred_element_type=F32)
    o_ref[0] = x0_ref[0] * (y + z_ref[0] * bias_ref[...])


def _dft_final(fb, d3, z3, x03, bias_t):
    b, m2, cols = d3.shape
    r = fb.shape[0]
    tn = min(4096, cols)
    rspec = pl.BlockSpec((1, r, tn), lambda bi, j: (bi, 0, j))
    return pl.pallas_call(
        _inv_kernel,
        grid=(b, cols // tn),
        in_specs=[pl.BlockSpec((r, m2), lambda bi, j: (0, 0)),
                  pl.BlockSpec((1, m2, tn), lambda bi, j: (bi, 0, j)),
                  rspec, rspec,
                  pl.BlockSpec((1, tn), lambda bi, j: (0, j))],
        out_specs=rspec,
        out_shape=jax.ShapeDtypeStruct((b, r, cols), F32),
        compiler_params=_cp(("parallel", "parallel")),
        name="dft_final",
    )(fb, d3, z3, x03, bias_t)


def _hyena_mix(z, x0, hy_bias, hf):
    b, l, c = z.shape
    n2 = DFT_N2
    n1s = 2 * l // n2
    fa, g, gi, fb = _dft_consts(l)
    hk, asum = _hyena_filter(l, *hf)
    cols = n2 * c
    ka = _dft_stage1(fa, hk.reshape(2, n1s // 2, cols))
    kf = _filter_spectrum(ka.reshape(2, 2, n1s, n2, c), g, asum)
    z3 = z.reshape(b, n1s // 2, cols)
    a = _dft_stage1(fa, z3)
    d = _spectral_multiply(a.reshape(b, 2, n1s, n2, c), g, kf, gi)
    bias_t = jnp.tile(hy_bias.astype(F32), n2)[None, :]
    yh = _dft_final(fb, d.reshape(b, 2 * n1s, cols), z3, x0.reshape(b, n1s // 2, cols), bias_t)
    return yh.reshape(b, l, c)


def _layer_norm(r, g, b):
    mu = jnp.mean(r, axis=-1, keepdims=True)
    rc = r - mu
    var = jnp.mean(rc * rc, axis=-1, keepdims=True)
    return rc * lax.rsqrt(var + LN_EPS) * g + b


def _out_proj_kernel(ya_ref, yh_ref, x_ref, wa_ref, wh_ref, g_ref, b_ref, o_ref, *, alpha):
    mix = (jnp.dot(ya_ref[...].astype(BF16), wa_ref[...], preferred_element_type=F32)
           + jnp.dot(yh_ref[...].astype(BF16), wh_ref[...], preferred_element_type=F32))
    o_ref[...] = _layer_norm(alpha * x_ref[...] + mix, g_ref[...], b_ref[...])


def _out_proj(ya, yh, x2, wa_bf, wh_bf, g, b, alpha):
    m, d = x2.shape
    cw = ya.shape[1]
    tm = min(512, m)
    row = lambda w: pl.BlockSpec((tm, w), lambda i: (i, 0))
    const = lambda shape: pl.BlockSpec(shape, lambda i: (0, 0))
    return pl.pallas_call(
        functools.partial(_out_proj_kernel, alpha=alpha),
        grid=(m // tm,),
        in_specs=[row(cw), row(cw), row(d), const((cw, d)), const((cw, d)), const((1, d)), const((1, d))],
        out_specs=row(d),
        out_shape=jax.ShapeDtypeStruct((m, d), F32),
        compiler_params=_cp(("parallel",)),
        name="out_proj_ln",
    )(ya, yh, x2, wa_bf, wh_bf, g, b)


NEG_INF = float("-inf")
NO_ROW = float(2 ** 30)


def _topk_keys(s, k):
    n, t = s.shape
    sub = lax.broadcasted_iota(jnp.int32, (SUBLANES, t), 0).astype(F32)
    slabs = [s[g * SUBLANES:(g + 1) * SUBLANES] for g in range(n // SUBLANES)]
    ids = [sub + float(g * SUBLANES) for g in range(n // SUBLANES)]
    vals, poss = [], []
    for _ in range(k):
        vs, rs = slabs, ids
        while len(vs) > 1:
            nv, nr = [], []
            for a in range(0, len(vs), 2):
                keep = vs[a] >= vs[a + 1]
                nv.append(jnp.maximum(vs[a], vs[a + 1]))
                nr.append(jnp.where(keep, rs[a], rs[a + 1]))
            vs, rs = nv, nr
        m = jnp.max(vs[0], axis=0, keepdims=True)
        pos = jnp.min(jnp.where(vs[0] == m, rs[0], NO_ROW), axis=0, keepdims=True)
        vals.append(m)
        poss.append(pos)
        slabs = [jnp.where(r == pos, NEG_INF, v) for v, r in zip(slabs, ids)]
    return jnp.concatenate(vals, axis=0), jnp.concatenate(poss, axis=0)


def _topk_ordered(s, order, payload, k):
    vals, pays = [], []
    for _ in range(k):
        m = jnp.max(s, axis=0, keepdims=True)
        first = jnp.min(jnp.where(s == m, order, NO_ROW), axis=0, keepdims=True)
        hit = order == first
        vals.append(m)
        pays.append(jnp.max(jnp.where(hit, payload, -1.0), axis=0, keepdims=True))
        s = jnp.where(hit, NEG_INF, s)
    return jnp.concatenate(vals, axis=0), jnp.concatenate(pays, axis=0)


def _pair_candidates(v1, i1, v2, i2, nkeys):
    kk, t = v1.shape
    up8 = lambda x: -(-x // SUBLANES) * SUBLANES
    lim = lambda a: kk // (a + 1)
    nfull = sum(1 for a in range(kk) if lim(a) >= SUBLANES)
    blocks = [("a", a, 0, lim(a)) for a in range(nfull)]
    blocks += [("b", b, nfull, lim(b)) for b in range(nfull)]
    blocks += [("a", a, nfull, lim(a)) for a in range(nfull, kk) if lim(a) > nfull]
    assert sum(hi - lo for _, _, lo, hi in blocks) == sum(lim(a) for a in range(kk))
    sums, poss, eids = [], [], []
    for kind, fixed, lo, hi in blocks:
        rows = up8(hi)
        it = lax.broadcasted_iota(jnp.int32, (rows, t), 0).astype(F32)
        valid = (it >= float(lo)) & (it < float(hi))
        if kind == "a":
            sm = v1[fixed:fixed + 1] + v2[:rows]
            ps = it + float(fixed * kk)
            ei = i1[fixed:fixed + 1] * float(nkeys) + i2[:rows]
        else:
            sm = v1[:rows] + v2[fixed:fixed + 1]
            ps = it * float(kk) + float(fixed)
            ei = i1[:rows] * float(nkeys) + i2[fixed:fixed + 1]
        sums.append(jnp.where(valid, sm, NEG_INF))
        poss.append(ps)
        eids.append(ei)
    cat = lambda xs: jnp.concatenate(xs, axis=0)
    return cat(sums), cat(poss), cat(eids)


def _route_kernel(x_ref, wq_ref, keys_ref, eidx_ref, gate_ref, q_scr, e_scr, g_scr, *, heads, nkeys, row_scale):
    kk = PEER_TOPK
    half = keys_ref.shape[2]
    q_scr[...] = jnp.dot(x_ref[...].astype(BF16), wq_ref[...], preferred_element_type=F32)

    def head(h, carry):
        tops = []
        for p in range(2):
            c = h * 2 + p
            qc = q_scr[:, pl.ds(pl.multiple_of(c * half, half), half)].astype(BF16)
            st = lax.dot_general(keys_ref[c], qc, (((1,), (1,)), ((), ())),
                                 preferred_element_type=F32)
            tops.append(_topk_keys(st, kk))
        (v1, i1), (v2, i2) = tops
        cand, cpos, cidx = _pair_candidates(v1, i1, v2, i2, nkeys)
        best, e = _topk_ordered(cand, cpos, cidx, kk)
        ex = jnp.exp(best - jnp.max(best, axis=0, keepdims=True))
        gate = ex / jnp.sum(ex, axis=0, keepdims=True)
        r0 = pl.multiple_of(h * kk, kk)
        e_scr[pl.ds(r0, kk), :] = e
        g_scr[pl.ds(r0, kk), :] = gate
        return carry

    lax.fori_loop(0, heads, head, 0)
    eidx_ref[...] = (e_scr[...].T * float(row_scale)).astype(jnp.int32)
    gate_ref[...] = g_scr[...].T


def _route(x2, wq_bf, keys_bf, heads, nkeys):
    m, d = x2.shape
    qd = wq_bf.shape[1]
    tt = min(256, m)
    hk = heads * PEER_TOPK
    return pl.pallas_call(
        functools.partial(_route_kernel, heads=heads, nkeys=nkeys, row_scale=PACKED_ROWS),
        grid=(m // tt,),
        in_specs=[pl.BlockSpec((tt, d), lambda i: (i, 0)),
                  pl.BlockSpec((d, qd), lambda i: (0, 0)),
                  pl.BlockSpec(keys_bf.shape, lambda i: (0, 0, 0))],
        out_specs=[pl.BlockSpec((tt, hk), lambda i: (i, 0)),
                   pl.BlockSpec((tt, hk), lambda i: (i, 0))],
        out_shape=[jax.ShapeDtypeStruct((m, hk), jnp.int32), jax.ShapeDtypeStruct((m, hk), F32)],
        scratch_shapes=[pltpu.VMEM((tt, qd), F32), pltpu.VMEM((hk, tt), F32), pltpu.VMEM((hk, tt), F32)],
        compiler_params=_cp(("parallel",)),
        name="peer_route",
    )(x2, wq_bf, keys_bf)


SUBLANES = 8
TOKEN_UNROLL = 8


def _pack_table(tab):
    e, d = tab.shape
    bits = lax.bitcast_convert_type(tab.astype(BF16).reshape(e, d // (2 * LANES), 2, LANES), jnp.uint16)
    bits = bits.astype(jnp.uint32)
    words = lax.bitcast_convert_type(bits[:, :, 0, :] | (bits[:, :, 1, :] << 16), jnp.int32)
    return words.reshape(e * (d // (2 * LANES)), LANES)


PACKED_ROWS = SUBLANES // 2


def _gather_rows(tab_ref, off_ref, t, nk):
    tiles = [tab_ref[pl.ds(pl.multiple_of(off_ref[t, k], PACKED_ROWS), PACKED_ROWS), :] for k in range(nk)]
    return pltpu.bitcast(jnp.concatenate(tiles, axis=0), BF16)


def _diag_mask(n):
    lane = lax.broadcasted_iota(jnp.int32, (SUBLANES, n), 1)
    return (lane & (SUBLANES - 1)) == lax.broadcasted_iota(jnp.int32, (SUBLANES, n), 0)


def _peer_u_kernel(idx_ref, x_ref, u_ref, o_ref):
    tt, nk = idx_ref.shape
    diag = _diag_mask(SUBLANES * nk)

    def toks(i, carry):
        for j in range(TOKEN_UNROLL):
            t = i * TOKEN_UNROLL + j
            r = _gather_rows(u_ref, idx_ref, t, nk)
            xt = x_ref[t].astype(BF16)
            p = lax.dot_general(xt, r, (((1,), (1,)), ((), ())), preferred_element_type=F32)
            o_ref[pl.ds(t, 1), :] = jnp.sum(jnp.where(diag, p, 0.0), axis=0, keepdims=True)
        return carry

    lax.fori_loop(0, tt // TOKEN_UNROLL, toks, 0)


def _peer_u(eidx, x3, u_pk):
    m, nk = eidx.shape
    ln = u_pk.shape[1]
    tt = min(256, m)
    n = SUBLANES * nk
    return pl.pallas_call(
        _peer_u_kernel,
        grid=(m // tt,),
        in_specs=[pl.BlockSpec((tt, nk), lambda i: (i, 0), memory_space=pltpu.SMEM),
                  pl.BlockSpec((tt, SUBLANES, ln), lambda i: (i, 0, 0)),
                  pl.BlockSpec(u_pk.shape, lambda i: (0, 0), pipeline_mode=pl.Buffered(1))],
        out_specs=pl.BlockSpec((tt, n), lambda i: (i, 0)),
        out_shape=jax.ShapeDtypeStruct((m, n), F32),
        compiler_params=_cp(("arbitrary",)),
        name="peer_u",
    )(eidx, x3, u_pk)


def _coef_kernel(part_ref, gate_ref, o_ref):
    n = part_ref.shape[1]
    nk = gate_ref.shape[1]
    sub = n // nk
    fold = (lax.broadcasted_iota(jnp.int32, (n, nk), 0) // sub
            == lax.broadcasted_iota(jnp.int32, (n, nk), 1)).astype(F32)
    spread = (lax.broadcasted_iota(jnp.int32, (nk, n), 1) // sub
              == lax.broadcasted_iota(jnp.int32, (nk, n), 0)).astype(F32)
    a = jnp.dot(part_ref[...], fold, precision=HI, preferred_element_type=F32)
    gelu = 0.5 * a * (1.0 + lax.erf(a * (1.0 / math.sqrt(2.0))))
    coef = gate_ref[...] * gelu
    o_ref[...] = jnp.dot(coef, spread, precision=HI, preferred_element_type=F32)


def _coef(part, gate):
    m, n = part.shape
    nk = gate.shape[1]
    tt = min(1024, m)
    return pl.pallas_call(
        _coef_kernel,
        grid=(m // tt,),
        in_specs=[pl.BlockSpec((tt, n), lambda i: (i, 0)),
                  pl.BlockSpec((tt, nk), lambda i: (i, 0))],
        out_specs=pl.BlockSpec((tt, n), lambda i: (i, 0)),
        out_shape=jax.ShapeDtypeStruct((m, n), F32),
        compiler_params=_cp(("parallel",)),
        name="peer_coef",
    )(part, gate)


def _peer_v_kernel(idx_ref, c_ref, v_ref, o_ref):
    tt, nk = idx_ref.shape
    diag = _diag_mask(SUBLANES * nk)

    def toks(i, carry):
        for j in range(TOKEN_UNROLL):
            t = i * TOKEN_UNROLL + j
            r = _gather_rows(v_ref, idx_ref, t, nk)
            c8 = jnp.where(diag, c_ref[pl.ds(t, 1), :], 0.0).astype(BF16)
            o_ref[t] = jnp.dot(c8, r, preferred_element_type=F32)
        return carry

    lax.fori_loop(0, tt // TOKEN_UNROLL, toks, 0)


def _peer_v(eidx, coef8, v_pk):
    m, nk = eidx.shape
    ln = v_pk.shape[1]
    tt = min(256, m)
    n = SUBLANES * nk
    return pl.pallas_call(
        _peer_v_kernel,
        grid=(m // tt,),
        in_specs=[pl.BlockSpec((tt, nk), lambda i: (i, 0), memory_space=pltpu.SMEM),
                  pl.BlockSpec((tt, n), lambda i: (i, 0)),
                  pl.BlockSpec(v_pk.shape, lambda i: (0, 0), pipeline_mode=pl.Buffered(1))],
        out_specs=pl.BlockSpec((tt, SUBLANES, ln), lambda i: (i, 0, 0)),
        out_shape=jax.ShapeDtypeStruct((m, SUBLANES, ln), F32),
        compiler_params=_cp(("arbitrary",)),
        name="peer_v",
    )(eidx, coef8, v_pk)


def _res_ln_kernel(x_ref, y_ref, g_ref, b_ref, o_ref, *, alpha):
    o_ref[...] = _layer_norm(alpha * x_ref[...] + y_ref[...], g_ref[...], b_ref[...])


def _res_ln(x2, y2, g, b, alpha):
    m, d = x2.shape
    tm = min(1024, m)
    row = pl.BlockSpec((tm, d), lambda i: (i, 0))
    const = pl.BlockSpec((1, d), lambda i: (0, 0))
    return pl.pallas_call(
        functools.partial(_res_ln_kernel, alpha=alpha),
        grid=(m // tm,),
        in_specs=[row, row, const, const],
        out_specs=row,
        out_shape=jax.ShapeDtypeStruct((m, d), F32),
        compiler_params=_cp(("parallel",)),
        name="residual_ln",
    )(x2, y2, g, b)


def _peer(x2, wq_bf, keys_bf, u_pk, v_pk, heads, nkeys):
    m, d = x2.shape
    assert d == SUBLANES * LANES, "the expert stages keep one token row per (8, 128) vreg"
    eidx, gate = _route(x2, wq_bf, keys_bf, heads, nkeys)
    part = _peer_u(eidx, x2.reshape(m, SUBLANES, LANES), u_pk)
    coef8 = _coef(part, gate)
    return _peer_v(eidx, coef8, v_pk).reshape(m, d)


def _encoder_layer(x, lw, alpha):
    b, l, d = x.shape
    cw = lw["a_conv_w"].shape[1]
    x2 = x.reshape(b * l, d)
    p = _in_proj(x2, lw["w_in"], lw["b_in"])
    ya, x0, z = _mix(p.reshape(b, l, -1), lw["a_conv_w"], lw["h_conv_w"], lw["h_conv_b"], cw)
    yh = _hyena_mix(z, x0, lw["hy_bias"], lw["hf"])
    x1 = _out_proj(ya.reshape(b * l, cw), yh.reshape(b * l, -1), x2, lw["w_out_a"], lw["w_out_h"],
                   lw["ln1_g"], lw["ln1_b"], alpha)
    y = _peer(x1, lw["peer_wq"], lw["peer_keys"], lw["peer_u"], lw["peer_v"], lw["heads"], lw["nkeys"])
    return _res_ln(x1, y, lw["ln2_g"], lw["ln2_b"], alpha).reshape(b, l, d)


def kernel(x_prompt, x_sample, w_in, b_in, a_conv_w, h_conv_w, h_conv_b, hf_w1, hf_b1, hf_freq, hf_w2,
           hf_b2, hf_w3, hy_decay, hy_bias, w_out, ln1_g, ln1_b, peer_wq, peer_keys, peer_u, peer_v,
           ln2_g, ln2_b):
    depth = w_in.shape[0]
    alpha = (2.0 * depth) ** 0.25
    layers = []
    for i in range(depth):
        cw = a_conv_w.shape[2]
        heads, _, nkeys, half = peer_keys.shape[1:]
        layers.append(dict(
            w_in=w_in[i].astype(BF16), b_in=b_in[i][None].astype(F32),
            a_conv_w=a_conv_w[i], h_conv_w=h_conv_w[i], h_conv_b=h_conv_b[i][None],
            hf=(hf_w1[i], hf_b1[i], hf_freq[i], hf_w2[i], hf_b2[i], hf_w3[i], hy_decay[i]),
            hy_bias=hy_bias[i],
            w_out_a=w_out[i, :cw].astype(BF16), w_out_h=w_out[i, cw:].astype(BF16),
            ln1_g=ln1_g[i][None], ln1_b=ln1_b[i][None],
            peer_wq=peer_wq[i].astype(BF16),
            peer_keys=peer_keys[i].reshape(heads * 2, nkeys, half).astype(BF16),
            peer_u=_pack_table(peer_u[i]), peer_v=_pack_table(peer_v[i]),
            ln2_g=ln2_g[i][None], ln2_b=ln2_b[i][None], heads=heads, nkeys=nkeys))

    def trunk(x):
        for lw in layers:
            x = _encoder_layer(x, lw, alpha)
        return x

    return trunk(x_prompt), trunk(x_sample)
```

```python
import functools
import math

import jax
import jax.numpy as jnp
from jax import lax
from jax.experimental import pallas as pl
from jax.experimental.pallas import tpu as pltpu

F32 = jnp.float32
BF16 = jnp.bfloat16
HI = lax.Precision.HIGHEST

LANES = 128
SHORT_K = 3
PEER_TOPK = 16
LN_EPS = 1e-5
DFT_N2 = 128
VMEM_LIMIT = 56 * 1024 * 1024


def _cp(sem, vmem=VMEM_LIMIT):
    return pltpu.CompilerParams(dimension_semantics=sem, vmem_limit_bytes=vmem)


def _in_proj_kernel(x_ref, w_ref, b_ref, o_ref):
    o_ref[...] = jnp.dot(x_ref[...].astype(BF16), w_ref[...],
                         preferred_element_type=F32) + b_ref[...]


def _in_proj(x2, w_bf, b):
    m, d = x2.shape
    n = w_bf.shape[1]
    tm = min(512, m)
    return pl.pallas_call(
        _in_proj_kernel,
        grid=(m // tm,),
        in_specs=[pl.BlockSpec((tm, d), lambda i: (i, 0)),
                  pl.BlockSpec((d, n), lambda i: (0, 0)),
                  pl.BlockSpec((1, n), lambda i: (0, 0))],
        out_specs=pl.BlockSpec((tm, n), lambda i: (i, 0)),
        out_shape=jax.ShapeDtypeStruct((m, n), F32),
        compiler_params=_cp(("parallel",)),
        name="in_proj",
    )(x2, w_bf, b)


def _mix_kernel(pm_ref, pp_ref, pn_ref, aw_ref, hw_ref, hb_ref, ya_ref, x0_ref, z_ref, *, cw):
    i = pl.program_id(1)
    n = pl.num_programs(1)
    tl = pm_ref.shape[1]
    rows = lax.broadcasted_iota(jnp.int32, (tl, 1), 0)
    has_prev = i > 0
    has_next = i < n - 1

    def conv3(cur, prev_row, next_row, w):
        prev_row = jnp.where(has_prev, prev_row, 0.0)
        next_row = jnp.where(has_next, next_row, 0.0)
        dn = jnp.where(rows == 0, prev_row, pltpu.roll(cur, 1, 0))
        up = jnp.where(rows == tl - 1, next_row, pltpu.roll(cur, tl - 1, 0))
        return dn * w[0:1] + cur * w[1:2] + up * w[2:3]

    def sec(ref, r0, r1, j):
        return ref[0, r0:r1, j * cw:(j + 1) * cw]

    g = sec(pm_ref, 0, tl, 1) * sec(pm_ref, 0, tl, 2)
    gp = sec(pp_ref, 7, 8, 1) * sec(pp_ref, 7, 8, 2)
    gn = sec(pn_ref, 0, 1, 1) * sec(pn_ref, 0, 1, 2)
    ya_ref[0] = sec(pm_ref, 0, tl, 0) * conv3(g, gp, gn, aw_ref[...])

    def hconv(j):
        c = 3 + j
        w = hw_ref[:, j * cw:(j + 1) * cw]
        return conv3(sec(pm_ref, 0, tl, c), sec(pp_ref, 7, 8, c), sec(pn_ref, 0, 1, c), w) \
            + hb_ref[:, j * cw:(j + 1) * cw]

    x0_ref[0] = hconv(0)
    z_ref[0] = hconv(2) * hconv(1)


def _mix(p3, a_conv_w, h_conv_w, h_conv_b, cw):
    b, l, pw = p3.shape
    tl = min(512, l)
    nl = l // tl
    r8 = tl // 8
    out = jax.ShapeDtypeStruct((b, l, cw), F32)
    ospec = pl.BlockSpec((1, tl, cw), lambda bi, i: (bi, i, 0))
    return pl.pallas_call(
        functools.partial(_mix_kernel, cw=cw),
        grid=(b, nl),
        in_specs=[pl.BlockSpec((1, tl, pw), lambda bi, i: (bi, i, 0)),
                  pl.BlockSpec((1, 8, pw), lambda bi, i: (bi, jnp.maximum(i * r8 - 1, 0), 0)),
                  pl.BlockSpec((1, 8, pw), lambda bi, i: (bi, jnp.minimum((i + 1) * r8, l // 8 - 1), 0)),
                  pl.BlockSpec((SHORT_K, cw), lambda bi, i: (0, 0)),
                  pl.BlockSpec((SHORT_K, 3 * cw), lambda bi, i: (0, 0)),
                  pl.BlockSpec((1, 3 * cw), lambda bi, i: (0, 0))],
        out_specs=[ospec, ospec, ospec],
        out_shape=[out, out, out],
        compiler_params=_cp(("parallel", "parallel")),
        name="mix",
    )(p3, p3, p3, a_conv_w, h_conv_w, h_conv_b)


def _filter_kernel(t_ref, w_ref, f_ref, w1t_ref, w1c_ref, w1s_ref, b1_ref, fr_ref, w2_ref, b2_ref,
                   w3_ref, dec_ref, hk_ref, asum_ref, *, hw):
    i = pl.program_id(0)
    tl = t_ref.shape[0]
    t = t_ref[...]
    ang = w_ref[...] * f_ref[...]
    z1 = (t * w1t_ref[...]
          + jnp.dot(jnp.cos(ang), w1c_ref[...], precision=HI, preferred_element_type=F32)
          + jnp.dot(-jnp.sin(ang), w1s_ref[...], precision=HI, preferred_element_type=F32)
          + b1_ref[...])
    fr = fr_ref[...]
    h = jnp.sin(fr * z1)
    h = jnp.sin(fr * (jnp.dot(h, w2_ref[...], precision=HI, preferred_element_type=F32) + b2_ref[...]))
    h = jnp.dot(h, w3_ref[...], precision=HI, preferred_element_type=F32)
    h = h * jnp.exp(-t * jnp.abs(dec_ref[...]))
    rows = i * tl + lax.broadcasted_iota(jnp.int32, (tl, 1), 0)
    fwd = h[:, :hw]
    bwd = jnp.where(rows == 0, 0.0, h[:, hw:])
    hk_ref[0] = fwd
    hk_ref[1] = bwd
    s = jnp.concatenate([jnp.sum(jnp.abs(fwd), axis=0, keepdims=True),
                         jnp.sum(jnp.abs(bwd), axis=0, keepdims=True)], axis=0)

    @pl.when(i == 0)
    def _():
        asum_ref[...] = jnp.zeros_like(asum_ref)

    asum_ref[...] += s


def _pad2(a, r, c):
    return jnp.zeros((r, c), F32).at[:a.shape[0], :a.shape[1]].set(a.astype(F32))


def _hyena_filter(l, hf_w1, hf_b1, hf_freq, hf_w2, hf_b2, hf_w3, hy_decay):
    bands = (hf_w1.shape[0] - 1) // 2
    hid = hf_w1.shape[1]
    hw = hy_decay.shape[1]
    t = jnp.linspace(0.0, 1.0, l, dtype=F32)[:, None]
    w = (2.0 * math.pi) * jnp.arange(l, dtype=F32)[:, None] / l
    f = jnp.linspace(1e-4, bands - 1, bands, dtype=F32)[None, :]
    p = LANES
    tl = min(512, l)
    args = (t, w, _pad2(f, 1, p), _pad2(hf_w1[0:1], 1, p), _pad2(hf_w1[1:1 + bands], p, p),
            _pad2(hf_w1[1 + bands:], p, p), _pad2(hf_b1[None], 1, p), _pad2(hf_freq[None], 1, p),
            _pad2(hf_w2, p, p), _pad2(hf_b2[None], 1, p), _pad2(hf_w3, p, 2 * hw),
            hy_decay.reshape(1, 2 * hw).astype(F32))
    del hid
    const = lambda shape: pl.BlockSpec(shape, lambda i: (0, 0))
    return pl.pallas_call(
        functools.partial(_filter_kernel, hw=hw),
        grid=(l // tl,),
        in_specs=[pl.BlockSpec((tl, 1), lambda i: (i, 0)), pl.BlockSpec((tl, 1), lambda i: (i, 0)),
                  const((1, p)), const((1, p)), const((p, p)), const((p, p)), const((1, p)),
                  const((1, p)), const((p, p)), const((1, p)), const((p, 2 * hw)), const((1, 2 * hw))],
        out_specs=[pl.BlockSpec((2, tl, hw), lambda i: (0, i, 0)),
                   pl.BlockSpec((2, hw), lambda i: (0, 0))],
        out_shape=[jax.ShapeDtypeStruct((2, l, hw), F32), jax.ShapeDtypeStruct((2, hw), F32)],
        compiler_params=_cp(("arbitrary",)),
        name="hyena_filter",
    )(*args)


def _dft_consts(l):
    n2s = DFT_N2
    n = 2 * l
    n1s = n // n2s
    two_pi = 2.0 * math.pi
    k1 = jnp.arange(n1s, dtype=jnp.int32)[:, None]
    n1 = jnp.arange(n1s // 2, dtype=jnp.int32)[None, :]
    ang = -two_pi * ((k1 * n1) % n1s).astype(F32) / n1s
    fa = jnp.concatenate([jnp.cos(ang), jnp.sin(ang)], axis=0)
    k1 = jnp.arange(n1s, dtype=jnp.int32)[:, None, None]
    k2 = jnp.arange(n2s, dtype=jnp.int32)[None, :, None]
    n2 = jnp.arange(n2s, dtype=jnp.int32)[None, None, :]
    ang = -two_pi * ((n2 * k2 * n1s + n2 * k1) % n).astype(F32) / n
    mr, mi = jnp.cos(ang), jnp.sin(ang)
    g = jnp.concatenate([jnp.concatenate([mr, -mi], 2), jnp.concatenate([mi, mr], 2)], 1)
    mrt, mit = jnp.swapaxes(mr, 1, 2), jnp.swapaxes(mi, 1, 2)
    gi = jnp.concatenate([jnp.concatenate([mrt, mit], 2), jnp.concatenate([-mit, mrt], 2)], 1)
    n1 = jnp.arange(n1s // 2, dtype=jnp.int32)[:, None]
    k1 = jnp.arange(n1s, dtype=jnp.int32)[None, :]
    ang = two_pi * ((n1 * k1) % n1s).astype(F32) / n1s
    fb = jnp.concatenate([jnp.cos(ang), -jnp.sin(ang)], axis=1) / n
    return fa, g, gi, fb


def _lmat_kernel(f_ref, z_ref, o_ref):
    o_ref[0] = jnp.dot(f_ref[...], z_ref[0], precision=HI, preferred_element_type=F32)


def _dft_stage1(fa, z3):
    b, r, cols = z3.shape
    m = fa.shape[0]
    tn = min(4096, cols)
    return pl.pallas_call(
        _lmat_kernel,
        grid=(b, cols // tn),
        in_specs=[pl.BlockSpec((m, r), lambda bi, j: (0, 0)),
                  pl.BlockSpec((1, r, tn), lambda bi, j: (bi, 0, j))],
        out_specs=pl.BlockSpec((1, m, tn), lambda bi, j: (bi, 0, j)),
        out_shape=jax.ShapeDtypeStruct((b, m, cols), F32),
        compiler_params=_cp(("parallel", "parallel")),
        name="dft_stage1",
    )(fa, z3)


def _kf_kernel(a_ref, g_ref, asum_ref, kf_ref):
    n2 = a_ref.shape[3]
    c = a_ref.shape[4]
    g = g_ref[0]
    f0 = jnp.dot(g, a_ref[0, :, 0].reshape(2 * n2, c), precision=HI, preferred_element_type=F32)
    f1 = jnp.dot(g, a_ref[1, :, 0].reshape(2 * n2, c), precision=HI, preferred_element_type=F32)
    inv = 1.0 / (asum_ref[0:1] + asum_ref[1:2])
    kf_ref[0] = jnp.concatenate([(f0[:n2] + f1[:n2]) * inv, (f0[n2:] - f1[n2:]) * inv], axis=0)


def _filter_spectrum(a5, g, asum):
    _, _, n1s, n2, c = a5.shape
    return pl.pallas_call(
        _kf_kernel,
        grid=(n1s,),
        in_specs=[pl.BlockSpec((2, 2, 1, n2, c), lambda k: (0, 0, k, 0, 0)),
                  pl.BlockSpec((1, 2 * n2, 2 * n2), lambda k: (k, 0, 0)),
                  pl.BlockSpec((2, c), lambda k: (0, 0))],
        out_specs=pl.BlockSpec((1, 2 * n2, c), lambda k: (k, 0, 0)),
        out_shape=jax.ShapeDtypeStruct((n1s, 2 * n2, c), F32),
        compiler_params=_cp(("parallel",)),
        name="filter_spectrum",
    )(a5, g, asum)


def _spec_kernel(a_ref, g_ref, kf_ref, gi_ref, d_ref):
    n2 = a_ref.shape[3]
    c = a_ref.shape[4]
    a = a_ref[0, :, 0].reshape(2 * n2, c)
    xf = jnp.dot(g_ref[0], a, precision=HI, preferred_element_type=F32)
    kf = kf_ref[0]
    xr, xi, kr, ki = xf[:n2], xf[n2:], kf[:n2], kf[n2:]
    y = jnp.concatenate([xr * kr - xi * ki, xr * ki + xi * kr], axis=0)
    d = jnp.dot(gi_ref[0], y, precision=HI, preferred_element_type=F32)
    d_ref[0, :, 0] = d.reshape(2, n2, c)


def _spectral_multiply(a5, g, kf, gi):
    b, _, n1s, n2, c = a5.shape
    blk = pl.BlockSpec((1, 2, 1, n2, c), lambda k, bi: (bi, 0, k, 0, 0))
    mat = pl.BlockSpec((1, 2 * n2, 2 * n2), lambda k, bi: (k, 0, 0))
    return pl.pallas_call(
        _spec_kernel,
        grid=(n1s, b),
        in_specs=[blk, mat, pl.BlockSpec((1, 2 * n2, c), lambda k, bi: (k, 0, 0)), mat],
        out_specs=blk,
        out_shape=jax.ShapeDtypeStruct(a5.shape, F32),
        compiler_params=_cp(("parallel", "arbitrary")),
        name="spectral_multiply",
    )(a5, g, kf, gi)


def _inv_kernel(f_ref, d_ref, z_ref, x0_ref, bias_ref, o_ref):
    y = jnp.dot(f_ref[...], d_ref[0], precision=HI, preferred_element_type=F32)
    o_ref[0] = x0_ref[0] * (y + z_ref[0] * bias_ref[...])


def _dft_final(fb, d3, z3, x03, bias_t):
    b, m2, cols = d3.shape
    r = fb.shape[0]
    tn = min(4096, cols)
    rspec = pl.BlockSpec((1, r, tn), lambda bi, j: (bi, 0, j))
    return pl.pallas_call(
        _inv_kernel,
        grid=(b, cols // tn),
        in_specs=[pl.BlockSpec((r, m2), lambda bi, j: (0, 0)),
                  pl.BlockSpec((1, m2, tn), lambda bi, j: (bi, 0, j)),
                  rspec, rspec,
                  pl.BlockSpec((1, tn), lambda bi, j: (0, j))],
        out_specs=rspec,
        out_shape=jax.ShapeDtypeStruct((b, r, cols), F32),
        compiler_params=_cp(("parallel", "parallel")),
        name="dft_final",
    )(fb, d3, z3, x03, bias_t)


def _hyena_mix(z, x0, hy_bias, hf):
    b, l, c = z.shape
    n2 = DFT_N2
    n1s = 2 * l // n2
    fa, g, gi, fb = _dft_consts(l)
    hk, asum = _hyena_filter(l, *hf)
    cols = n2 * c
    ka = _dft_stage1(fa, hk.reshape(2, n1s // 2, cols))
    kf = _filter_spectrum(ka.reshape(2, 2, n1s, n2, c), g, asum)
    z3 = z.reshape(b, n1s // 2, cols)
    a = _dft_stage1(fa, z3)
    d = _spectral_multiply(a.reshape(b, 2, n1s, n2, c), g, kf, gi)
    bias_t = jnp.tile(hy_bias.astype(F32), n2)[None, :]
    yh = _dft_final(fb, d.reshape(b, 2 * n1s, cols), z3, x0.reshape(b, n1s // 2, cols), bias_t)
    return yh.reshape(b, l, c)


def _layer_norm(r, g, b):
    mu = jnp.mean(r, axis=-1, keepdims=True)
    rc = r - mu
    var = jnp.mean(rc * rc, axis=-1, keepdims=True)
    return rc * lax.rsqrt(var + LN_EPS) * g + b


def _out_proj_kernel(ya_ref, yh_ref, x_ref, wa_ref, wh_ref, g_ref, b_ref, o_ref, *, alpha):
    mix = (jnp.dot(ya_ref[...].astype(BF16), wa_ref[...], preferred_element_type=F32)
           + jnp.dot(yh_ref[...].astype(BF16), wh_ref[...], preferred_element_type=F32))
    o_ref[...] = _layer_norm(alpha * x_ref[...] + mix, g_ref[...], b_ref[...])


def _out_proj(ya, yh, x2, wa_bf, wh_bf, g, b, alpha):
    m, d = x2.shape
    cw = ya.shape[1]
    tm = min(512, m)
    row = lambda w: pl.BlockSpec((tm, w), lambda i: (i, 0))
    const = lambda shape: pl.BlockSpec(shape, lambda i: (0, 0))
    return pl.pallas_call(
        functools.partial(_out_proj_kernel, alpha=alpha),
        grid=(m // tm,),
        in_specs=[row(cw), row(cw), row(d), const((cw, d)), const((cw, d)), const((1, d)), const((1, d))],
        out_specs=row(d),
        out_shape=jax.ShapeDtypeStruct((m, d), F32),
        compiler_params=_cp(("parallel",)),
        name="out_proj_ln",
    )(ya, yh, x2, wa_bf, wh_bf, g, b)


NEG_INF = float("-inf")
NO_ROW = float(2 ** 30)


def _topk_keys(s, k):
    n, t = s.shape
    sub = lax.broadcasted_iota(jnp.int32, (SUBLANES, t), 0).astype(F32)
    slabs = [s[g * SUBLANES:(g + 1) * SUBLANES] for g in range(n // SUBLANES)]
    ids = [sub + float(g * SUBLANES) for g in range(n // SUBLANES)]
    vals, poss = [], []
    for _ in range(k):
        vs, rs = slabs, ids
        while len(vs) > 1:
            nv, nr = [], []
            for a in range(0, len(vs), 2):
                keep = vs[a] >= vs[a + 1]
                nv.append(jnp.maximum(vs[a], vs[a + 1]))
                nr.append(jnp.where(keep, rs[a], rs[a + 1]))
            vs, rs = nv, nr
        m = jnp.max(vs[0], axis=0, keepdims=True)
        pos = jnp.min(jnp.where(vs[0] == m, rs[0], NO_ROW), axis=0, keepdims=True)
        vals.append(m)
        poss.append(pos)
        slabs = [jnp.where(r == pos, NEG_INF, v) for v, r in zip(slabs, ids)]
    return jnp.concatenate(vals, axis=0), jnp.concatenate(poss, axis=0)


def _topk_ordered(s, order, payload, k):
    vals, pays = [], []
    for _ in range(k):
        m = jnp.max(s, axis=0, keepdims=True)
        first = jnp.min(jnp.where(s == m, order, NO_ROW), axis=0, keepdims=True)
        hit = order == first
        vals.append(m)
        pays.append(jnp.max(jnp.where(hit, payload, -1.0), axis=0, keepdims=True))
        s = jnp.where(hit, NEG_INF, s)
    return jnp.concatenate(vals, axis=0), jnp.concatenate(pays, axis=0)


def _pair_candidates(v1, i1, v2, i2, nkeys):
    kk, t = v1.shape
    up8 = lambda x: -(-x // SUBLANES) * SUBLANES
    lim = lambda a: kk // (a + 1)
    nfull = sum(1 for a in range(kk) if lim(a) >= SUBLANES)
    blocks = [("a", a, 0, lim(a)) for a in range(nfull)]
    blocks += [("b", b, nfull, lim(b)) for b in range(nfull)]
    blocks += [("a", a, nfull, lim(a)) for a in range(nfull, kk) if lim(a) > nfull]
    assert sum(hi - lo for _, _, lo, hi in blocks) == sum(lim(a) for a in range(kk))
    sums, poss, eids = [], [], []
    for kind, fixed, lo, hi in blocks:
        rows = up8(hi)
        it = lax.broadcasted_iota(jnp.int32, (rows, t), 0).astype(F32)
        valid = (it >= float(lo)) & (it < float(hi))
        if kind == "a":
            sm = v1[fixed:fixed + 1] + v2[:rows]
            ps = it + float(fixed * kk)
            ei = i1[fixed:fixed + 1] * float(nkeys) + i2[:rows]
        else:
            sm = v1[:rows] + v2[fixed:fixed + 1]
            ps = it * float(kk) + float(fixed)
            ei = i1[:rows] * float(nkeys) + i2[fixed:fixed + 1]
        sums.append(jnp.where(valid, sm, NEG_INF))
        poss.append(ps)
        eids.append(ei)
    cat = lambda xs: jnp.concatenate(xs, axis=0)
    return cat(sums), cat(poss), cat(eids)


def _route_kernel(x_ref, wq_ref, keys_ref, eidx_ref, gate_ref, q_scr, e_scr, g_scr, *, heads, nkeys, row_scale):
    kk = PEER_TOPK
    half = keys_ref.shape[2]
    q_scr[...] = jnp.dot(x_ref[...].astype(BF16), wq_ref[...], preferred_element_type=F32)

    def head(h, carry):
        tops = []
        for p in range(2):
            c = h * 2 + p
            qc = q_scr[:, pl.ds(pl.multiple_of(c * half, half), half)].astype(BF16)
            st = lax.dot_general(keys_ref[c], qc, (((1,), (1,)), ((), ())),
                                 preferred_element_type=F32)
            tops.append(_topk_keys(st, kk))
        (v1, i1), (v2, i2) = tops
        cand, cpos, cidx = _pair_candidates(v1, i1, v2, i2, nkeys)
        best, e = _topk_ordered(cand, cpos, cidx, kk)
        ex = jnp.exp(best - jnp.max(best, axis=0, keepdims=True))
        gate = ex / jnp.sum(ex, axis=0, keepdims=True)
        r0 = pl.multiple_of(h * kk, kk)
        e_scr[pl.ds(r0, kk), :] = e
        g_scr[pl.ds(r0, kk), :] = gate
        return carry

    lax.fori_loop(0, heads, head, 0)
    eidx_ref[...] = (e_scr[...].T * float(row_scale)).astype(jnp.int32)
    gate_ref[...] = g_scr[...].T


def _route(x2, wq_bf, keys_bf, heads, nkeys):
    m, d = x2.shape
    qd = wq_bf.shape[1]
    tt = min(256, m)
    hk = heads * PEER_TOPK
    return pl.pallas_call(
        functools.partial(_route_kernel, heads=heads, nkeys=nkeys, row_scale=PACKED_ROWS),
        grid=(m // tt,),
        in_specs=[pl.BlockSpec((tt, d), lambda i: (i, 0)),
                  pl.BlockSpec((d, qd), lambda i: (0, 0)),
                  pl.BlockSpec(keys_bf.shape, lambda i: (0, 0, 0))],
        out_specs=[pl.BlockSpec((tt, hk), lambda i: (i, 0)),
                   pl.BlockSpec((tt, hk), lambda i: (i, 0))],
        out_shape=[jax.ShapeDtypeStruct((m, hk), jnp.int32), jax.ShapeDtypeStruct((m, hk), F32)],
        scratch_shapes=[pltpu.VMEM((tt, qd), F32), pltpu.VMEM((hk, tt), F32), pltpu.VMEM((hk, tt), F32)],
        compiler_params=_cp(("parallel",)),
        name="peer_route",
    )(x2, wq_bf, keys_bf)


SUBLANES = 8
EXPERT_TOKENS = 64


def _pack_table(tab):
    e, d = tab.shape
    bits = lax.bitcast_convert_type(tab.astype(BF16).reshape(e, d // (2 * LANES), 2, LANES), jnp.uint16)
    bits = bits.astype(jnp.uint32)
    words = lax.bitcast_convert_type(bits[:, :, 0, :] | (bits[:, :, 1, :] << 16), jnp.int32)
    return words.reshape(e * (d // (2 * LANES)), LANES)


PACKED_ROWS = SUBLANES // 2


def _gather_rows(tab_ref, off_ref, t, nk):
    tiles = [tab_ref[pl.ds(pl.multiple_of(off_ref[t, k], PACKED_ROWS), PACKED_ROWS), :] for k in range(nk)]
    return pltpu.bitcast(jnp.concatenate(tiles, axis=0), BF16)


def _diag_mask(n):
    lane = lax.broadcasted_iota(jnp.int32, (SUBLANES, n), 1)
    return (lane & (SUBLANES - 1)) == lax.broadcasted_iota(jnp.int32, (SUBLANES, n), 0)


def _with_offsets(off_hbm, bufs, sems, body):
    i = pl.program_id(0)
    n = pl.num_programs(0)
    tt = bufs[0].shape[0]

    def fetch(step, slot):
        return pltpu.make_async_copy(off_hbm.at[pl.ds(step * tt, tt)], bufs[slot], sems.at[slot])

    @pl.when(i == 0)
    def _():
        fetch(0, 0).start()

    for slot in range(2):
        @pl.when(i % 2 == slot)
        def _(slot=slot):
            @pl.when(i + 1 < n)
            def _():
                fetch(i + 1, 1 - slot).start()

            fetch(i, slot).wait()
            body(bufs[slot])


def _peer_u_kernel(off_hbm, x_ref, u_ref, o_ref, off_a, off_b, sems):
    tt, nk = off_a.shape
    diag = _diag_mask(SUBLANES * nk)

    def body(off_ref):
        for t in range(tt):
            r = _gather_rows(u_ref, off_ref, t, nk)
            xt = x_ref[t].astype(BF16)
            p = lax.dot_general(xt, r, (((1,), (1,)), ((), ())), preferred_element_type=F32)
            o_ref[pl.ds(t, 1), :] = jnp.sum(jnp.where(diag, p, 0.0), axis=0, keepdims=True)

    _with_offsets(off_hbm, (off_a, off_b), sems, body)


def _peer_u(eidx, x3, u_pk):
    m, nk = eidx.shape
    ln = u_pk.shape[1]
    tt = min(EXPERT_TOKENS, m)
    n = SUBLANES * nk
    return pl.pallas_call(
        _peer_u_kernel,
        grid=(m // tt,),
        in_specs=[pl.BlockSpec(memory_space=pl.ANY),
                  pl.BlockSpec((tt, SUBLANES, ln), lambda i: (i, 0, 0)),
                  pl.BlockSpec(u_pk.shape, lambda i: (0, 0), pipeline_mode=pl.Buffered(1))],
        out_specs=pl.BlockSpec((tt, n), lambda i: (i, 0)),
        out_shape=jax.ShapeDtypeStruct((m, n), F32),
        scratch_shapes=[pltpu.SMEM((tt, nk), jnp.int32), pltpu.SMEM((tt, nk), jnp.int32),
                        pltpu.SemaphoreType.DMA((2,))],
        compiler_params=_cp(("arbitrary",)),
        name="peer_u",
    )(eidx, x3, u_pk)


def _coef_kernel(part_ref, gate_ref, o_ref):
    n = part_ref.shape[1]
    nk = gate_ref.shape[1]
    sub = n // nk
    fold = (lax.broadcasted_iota(jnp.int32, (n, nk), 0) // sub
            == lax.broadcasted_iota(jnp.int32, (n, nk), 1)).astype(F32)
    spread = (lax.broadcasted_iota(jnp.int32, (nk, n), 1) // sub
              == lax.broadcasted_iota(jnp.int32, (nk, n), 0)).astype(F32)
    a = jnp.dot(part_ref[...], fold, precision=HI, preferred_element_type=F32)
    gelu = 0.5 * a * (1.0 + lax.erf(a * (1.0 / math.sqrt(2.0))))
    coef = gate_ref[...] * gelu
    o_ref[...] = jnp.dot(coef, spread, precision=HI, preferred_element_type=F32)


def _coef(part, gate):
    m, n = part.shape
    nk = gate.shape[1]
    tt = min(1024, m)
    return pl.pallas_call(
        _coef_kernel,
        grid=(m // tt,),
        in_specs=[pl.BlockSpec((tt, n), lambda i: (i, 0)),
                  pl.BlockSpec((tt, nk), lambda i: (i, 0))],
        out_specs=pl.BlockSpec((tt, n), lambda i: (i, 0)),
        out_shape=jax.ShapeDtypeStruct((m, n), F32),
        compiler_params=_cp(("parallel",)),
        name="peer_coef",
    )(part, gate)


def _peer_v_kernel(off_hbm, c_ref, v_ref, o_ref, off_a, off_b, sems):
    tt, nk = off_a.shape
    diag = _diag_mask(SUBLANES * nk)

    def body(off_ref):
        for t in range(tt):
            r = _gather_rows(v_ref, off_ref, t, nk)
            c8 = jnp.where(diag, c_ref[pl.ds(t, 1), :], 0.0).astype(BF16)
            o_ref[t] = jnp.dot(c8, r, preferred_element_type=F32)

    _with_offsets(off_hbm, (off_a, off_b), sems, body)


def _peer_v(eidx, coef8, v_pk):
    m, nk = eidx.shape
    ln = v_pk.shape[1]
    tt = min(EXPERT_TOKENS, m)
    n = SUBLANES * nk
    return pl.pallas_call(
        _peer_v_kernel,
        grid=(m // tt,),
        in_specs=[pl.BlockSpec(memory_space=pl.ANY),
                  pl.BlockSpec((tt, n), lambda i: (i, 0)),
                  pl.BlockSpec(v_pk.shape, lambda i: (0, 0), pipeline_mode=pl.Buffered(1))],
        out_specs=pl.BlockSpec((tt, SUBLANES, ln), lambda i: (i, 0, 0)),
        out_shape=jax.ShapeDtypeStruct((m, SUBLANES, ln), F32),
        scratch_shapes=[pltpu.SMEM((tt, nk), jnp.int32), pltpu.SMEM((tt, nk), jnp.int32),
                        pltpu.SemaphoreType.DMA((2,))],
        compiler_params=_cp(("arbitrary",)),
        name="peer_v",
    )(eidx, coef8, v_pk)


def _res_ln_kernel(x_ref, y_ref, g_ref, b_ref, o_ref, *, alpha):
    o_ref[...] = _layer_norm(alpha * x_ref[...] + y_ref[...], g_ref[...], b_ref[...])


def _res_ln(x2, y2, g, b, alpha):
    m, d = x2.shape
    tm = min(1024, m)
    row = pl.BlockSpec((tm, d), lambda i: (i, 0))
    const = pl.BlockSpec((1, d), lambda i: (0, 0))
    return pl.pallas_call(
        functools.partial(_res_ln_kernel, alpha=alpha),
        grid=(m // tm,),
        in_specs=[row, row, const, const],
        out_specs=row,
        out_shape=jax.ShapeDtypeStruct((m, d), F32),
        compiler_params=_cp(("parallel",)),
        name="residual_ln",
    )(x2, y2, g, b)


def _peer(x2, wq_bf, keys_bf, u_pk, v_pk, heads, nkeys):
    m, d = x2.shape
    assert d == SUBLANES * LANES, "the expert stages keep one token row per (8, 128) vreg"
    eidx, gate = _route(x2, wq_bf, keys_bf, heads, nkeys)
    part = _peer_u(eidx, x2.reshape(m, SUBLANES, LANES), u_pk)
    coef8 = _coef(part, gate)
    return _peer_v(eidx, coef8, v_pk).reshape(m, d)


def _encoder_layer(x, lw, alpha):
    b, l, d = x.shape
    cw = lw["a_conv_w"].shape[1]
    x2 = x.reshape(b * l, d)
    p = _in_proj(x2, lw["w_in"], lw["b_in"])
    ya, x0, z = _mix(p.reshape(b, l, -1), lw["a_conv_w"], lw["h_conv_w"], lw["h_conv_b"], cw)
    yh = _hyena_mix(z, x0, lw["hy_bias"], lw["hf"])
    x1 = _out_proj(ya.reshape(b * l, cw), yh.reshape(b * l, -1), x2, lw["w_out_a"], lw["w_out_h"],
                   lw["ln1_g"], lw["ln1_b"], alpha)
    y = _peer(x1, lw["peer_wq"], lw["peer_keys"], lw["peer_u"], lw["peer_v"], lw["heads"], lw["nkeys"])
    return _res_ln(x1, y, lw["ln2_g"], lw["ln2_b"], alpha).reshape(b, l, d)


def kernel(x_prompt, x_sample, w_in, b_in, a_conv_w, h_conv_w, h_conv_b, hf_w1, hf_b1, hf_freq, hf_w2,
           hf_b2, hf_w3, hy_decay, hy_bias, w_out, ln1_g, ln1_b, peer_wq, peer_keys, peer_u, peer_v,
           ln2_g, ln2_b):
    depth = w_in.shape[0]
    alpha = (2.0 * depth) ** 0.25
    layers = []
    for i in range(depth):
        cw = a_conv_w.shape[2]
        heads, _, nkeys, half = peer_keys.shape[1:]
        layers.append(dict(
            w_in=w_in[i].astype(BF16), b_in=b_in[i][None].astype(F32),
            a_conv_w=a_conv_w[i], h_conv_w=h_conv_w[i], h_conv_b=h_conv_b[i][None],
            hf=(hf_w1[i], hf_b1[i], hf_freq[i], hf_w2[i], hf_b2[i], hf_w3[i], hy_decay[i]),
            hy_bias=hy_bias[i],
            w_out_a=w_out[i, :cw].astype(BF16), w_out_h=w_out[i, cw:].astype(BF16),
            ln1_g=ln1_g[i][None], ln1_b=ln1_b[i][None],
            peer_wq=peer_wq[i].astype(BF16),
            peer_keys=peer_keys[i].reshape(heads * 2, nkeys, half).astype(BF16),
            peer_u=_pack_table(peer_u[i]), peer_v=_pack_table(peer_v[i]),
            ln2_g=ln2_g[i][None], ln2_b=ln2_b[i][None], heads=heads, nkeys=nkeys))

    def trunk(x):
        for lw in layers:
            x = _encoder_layer(x, lw, alpha)
        return x

    return trunk(x_prompt), trunk(x_sample)
```

```python
import functools
import math

import jax
import jax.numpy as jnp
from jax import lax
from jax.experimental import pallas as pl
from jax.experimental.pallas import tpu as pltpu

F32 = jnp.float32
BF16 = jnp.bfloat16
HI = lax.Precision.HIGHEST

LANES = 128
SHORT_K = 3
PEER_TOPK = 16
LN_EPS = 1e-5
DFT_N2 = 128
VMEM_LIMIT = 56 * 1024 * 1024


def _cp(sem, vmem=VMEM_LIMIT):
    return pltpu.CompilerParams(dimension_semantics=sem, vmem_limit_bytes=vmem)


def _in_proj_kernel(x_ref, w_ref, b_ref, o_ref):
    o_ref[...] = jnp.dot(x_ref[...].astype(BF16), w_ref[...],
                         preferred_element_type=F32) + b_ref[...]


def _in_proj(x2, w_bf, b):
    m, d = x2.shape
    n = w_bf.shape[1]
    tm = min(512, m)
    return pl.pallas_call(
        _in_proj_kernel,
        grid=(m // tm,),
        in_specs=[pl.BlockSpec((tm, d), lambda i: (i, 0)),
                  pl.BlockSpec((d, n), lambda i: (0, 0)),
                  pl.BlockSpec((1, n), lambda i: (0, 0))],
        out_specs=pl.BlockSpec((tm, n), lambda i: (i, 0)),
        out_shape=jax.ShapeDtypeStruct((m, n), F32),
        compiler_params=_cp(("parallel",)),
        name="in_proj",
    )(x2, w_bf, b)


def _mix_kernel(pm_ref, pp_ref, pn_ref, aw_ref, hw_ref, hb_ref, ya_ref, x0_ref, z_ref, *, cw):
    i = pl.program_id(1)
    n = pl.num_programs(1)
    tl = pm_ref.shape[1]
    rows = lax.broadcasted_iota(jnp.int32, (tl, 1), 0)
    has_prev = i > 0
    has_next = i < n - 1

    def conv3(cur, prev_row, next_row, w):
        prev_row = jnp.where(has_prev, prev_row, 0.0)
        next_row = jnp.where(has_next, next_row, 0.0)
        dn = jnp.where(rows == 0, prev_row, pltpu.roll(cur, 1, 0))
        up = jnp.where(rows == tl - 1, next_row, pltpu.roll(cur, tl - 1, 0))
        return dn * w[0:1] + cur * w[1:2] + up * w[2:3]

    def sec(ref, r0, r1, j):
        return ref[0, r0:r1, j * cw:(j + 1) * cw]

    g = sec(pm_ref, 0, tl, 1) * sec(pm_ref, 0, tl, 2)
    gp = sec(pp_ref, 7, 8, 1) * sec(pp_ref, 7, 8, 2)
    gn = sec(pn_ref, 0, 1, 1) * sec(pn_ref, 0, 1, 2)
    ya_ref[0] = sec(pm_ref, 0, tl, 0) * conv3(g, gp, gn, aw_ref[...])

    def hconv(j):
        c = 3 + j
        w = hw_ref[:, j * cw:(j + 1) * cw]
        return conv3(sec(pm_ref, 0, tl, c), sec(pp_ref, 7, 8, c), sec(pn_ref, 0, 1, c), w) \
            + hb_ref[:, j * cw:(j + 1) * cw]

    x0_ref[0] = hconv(0)
    z_ref[0] = hconv(2) * hconv(1)


def _mix(p3, a_conv_w, h_conv_w, h_conv_b, cw):
    b, l, pw = p3.shape
    tl = min(512, l)
    nl = l // tl
    r8 = tl // 8
    out = jax.ShapeDtypeStruct((b, l, cw), F32)
    ospec = pl.BlockSpec((1, tl, cw), lambda bi, i: (bi, i, 0))
    return pl.pallas_call(
        functools.partial(_mix_kernel, cw=cw),
        grid=(b, nl),
        in_specs=[pl.BlockSpec((1, tl, pw), lambda bi, i: (bi, i, 0)),
                  pl.BlockSpec((1, 8, pw), lambda bi, i: (bi, jnp.maximum(i * r8 - 1, 0), 0)),
                  pl.BlockSpec((1, 8, pw), lambda bi, i: (bi, jnp.minimum((i + 1) * r8, l // 8 - 1), 0)),
                  pl.BlockSpec((SHORT_K, cw), lambda bi, i: (0, 0)),
                  pl.BlockSpec((SHORT_K, 3 * cw), lambda bi, i: (0, 0)),
                  pl.BlockSpec((1, 3 * cw), lambda bi, i: (0, 0))],
        out_specs=[ospec, ospec, ospec],
        out_shape=[out, out, out],
        compiler_params=_cp(("parallel", "parallel")),
        name="mix",
    )(p3, p3, p3, a_conv_w, h_conv_w, h_conv_b)


def _filter_kernel(t_ref, w_ref, f_ref, w1t_ref, w1c_ref, w1s_ref, b1_ref, fr_ref, w2_ref, b2_ref,
                   w3_ref, dec_ref, hk_ref, asum_ref, *, hw):
    i = pl.program_id(0)
    tl = t_ref.shape[0]
    t = t_ref[...]
    ang = w_ref[...] * f_ref[...]
    z1 = (t * w1t_ref[...]
          + jnp.dot(jnp.cos(ang), w1c_ref[...], precision=HI, preferred_element_type=F32)
          + jnp.dot(-jnp.sin(ang), w1s_ref[...], precision=HI, preferred_element_type=F32)
          + b1_ref[...])
    fr = fr_ref[...]
    h = jnp.sin(fr * z1)
    h = jnp.sin(fr * (jnp.dot(h, w2_ref[...], precision=HI, preferred_element_type=F32) + b2_ref[...]))
    h = jnp.dot(h, w3_ref[...], precision=HI, preferred_element_type=F32)
    h = h * jnp.exp(-t * jnp.abs(dec_ref[...]))
    rows = i * tl + lax.broadcasted_iota(jnp.int32, (tl, 1), 0)
    fwd = h[:, :hw]
    bwd = jnp.where(rows == 0, 0.0, h[:, hw:])
    hk_ref[0] = fwd
    hk_ref[1] = bwd
    s = jnp.concatenate([jnp.sum(jnp.abs(fwd), axis=0, keepdims=True),
                         jnp.sum(jnp.abs(bwd), axis=0, keepdims=True)], axis=0)

    @pl.when(i == 0)
    def _():
        asum_ref[...] = jnp.zeros_like(asum_ref)

    asum_ref[...] += s


def _pad2(a, r, c):
    return jnp.zeros((r, c), F32).at[:a.shape[0], :a.shape[1]].set(a.astype(F32))


def _hyena_filter(l, hf_w1, hf_b1, hf_freq, hf_w2, hf_b2, hf_w3, hy_decay):
    bands = (hf_w1.shape[0] - 1) // 2
    hid = hf_w1.shape[1]
    hw = hy_decay.shape[1]
    t = jnp.linspace(0.0, 1.0, l, dtype=F32)[:, None]
    w = (2.0 * math.pi) * jnp.arange(l, dtype=F32)[:, None] / l
    f = jnp.linspace(1e-4, bands - 1, bands, dtype=F32)[None, :]
    p = LANES
    tl = min(512, l)
    args = (t, w, _pad2(f, 1, p), _pad2(hf_w1[0:1], 1, p), _pad2(hf_w1[1:1 + bands], p, p),
            _pad2(hf_w1[1 + bands:], p, p), _pad2(hf_b1[None], 1, p), _pad2(hf_freq[None], 1, p),
            _pad2(hf_w2, p, p), _pad2(hf_b2[None], 1, p), _pad2(hf_w3, p, 2 * hw),
            hy_decay.reshape(1, 2 * hw).astype(F32))
    del hid
    const = lambda shape: pl.BlockSpec(shape, lambda i: (0, 0))
    return pl.pallas_call(
        functools.partial(_filter_kernel, hw=hw),
        grid=(l // tl,),
        in_specs=[pl.BlockSpec((tl, 1), lambda i: (i, 0)), pl.BlockSpec((tl, 1), lambda i: (i, 0)),
                  const((1, p)), const((1, p)), const((p, p)), const((p, p)), const((1, p)),
                  const((1, p)), const((p, p)), const((1, p)), const((p, 2 * hw)), const((1, 2 * hw))],
        out_specs=[pl.BlockSpec((2, tl, hw), lambda i: (0, i, 0)),
                   pl.BlockSpec((2, hw), lambda i: (0, 0))],
        out_shape=[jax.ShapeDtypeStruct((2, l, hw), F32), jax.ShapeDtypeStruct((2, hw), F32)],
        compiler_params=_cp(("arbitrary",)),
        name="hyena_filter",
    )(*args)


def _dft_consts(l):
    n2s = DFT_N2
    n = 2 * l
    n1s = n // n2s
    two_pi = 2.0 * math.pi
    k1 = jnp.arange(n1s, dtype=jnp.int32)[:, None]
    n1 = jnp.arange(n1s // 2, dtype=jnp.int32)[None, :]
    ang = -two_pi * ((k1 * n1) % n1s).astype(F32) / n1s
    fa = jnp.concatenate([jnp.cos(ang), jnp.sin(ang)], axis=0)
    k1 = jnp.arange(n1s, dtype=jnp.int32)[:, None, None]
    k2 = jnp.arange(n2s, dtype=jnp.int32)[None, :, None]
    n2 = jnp.arange(n2s, dtype=jnp.int32)[None, None, :]
    ang = -two_pi * ((n2 * k2 * n1s + n2 * k1) % n).astype(F32) / n
    mr, mi = jnp.cos(ang), jnp.sin(ang)
    g = jnp.concatenate([jnp.concatenate([mr, -mi], 2), jnp.concatenate([mi, mr], 2)], 1)
    mrt, mit = jnp.swapaxes(mr, 1, 2), jnp.swapaxes(mi, 1, 2)
    gi = jnp.concatenate([jnp.concatenate([mrt, mit], 2), jnp.concatenate([-mit, mrt], 2)], 1)
    n1 = jnp.arange(n1s // 2, dtype=jnp.int32)[:, None]
    k1 = jnp.arange(n1s, dtype=jnp.int32)[None, :]
    ang = two_pi * ((n1 * k1) % n1s).astype(F32) / n1s
    fb = jnp.concatenate([jnp.cos(ang), -jnp.sin(ang)], axis=1) / n
    return fa, g, gi, fb


def _mm(a, b, precise):
    if precise:
        return jnp.dot(a, b, precision=HI, preferred_element_type=F32)
    return jnp.dot(a.astype(BF16), b.astype(BF16), preferred_element_type=F32)


def _lmat_kernel(f_ref, z_ref, o_ref, *, precise):
    o_ref[0] = _mm(f_ref[...], z_ref[0], precise).astype(o_ref.dtype)


def _dft_stage1(fa, z3, precise):
    b, r, cols = z3.shape
    m = fa.shape[0]
    tn = min(4096, cols)
    return pl.pallas_call(
        functools.partial(_lmat_kernel, precise=precise),
        grid=(b, cols // tn),
        in_specs=[pl.BlockSpec((m, r), lambda bi, j: (0, 0)),
                  pl.BlockSpec((1, r, tn), lambda bi, j: (bi, 0, j))],
        out_specs=pl.BlockSpec((1, m, tn), lambda bi, j: (bi, 0, j)),
        out_shape=jax.ShapeDtypeStruct((b, m, cols), F32 if precise else BF16),
        compiler_params=_cp(("parallel", "parallel")),
        name="dft_stage1",
    )(fa, z3)


def _kf_kernel(a_ref, g_ref, asum_ref, kf_ref):
    n2 = a_ref.shape[3]
    c = a_ref.shape[4]
    g = g_ref[0]
    f0 = jnp.dot(g, a_ref[0, :, 0].reshape(2 * n2, c), precision=HI, preferred_element_type=F32)
    f1 = jnp.dot(g, a_ref[1, :, 0].reshape(2 * n2, c), precision=HI, preferred_element_type=F32)
    inv = 1.0 / (asum_ref[0:1] + asum_ref[1:2])
    kf_ref[0] = jnp.concatenate([(f0[:n2] + f1[:n2]) * inv, (f0[n2:] - f1[n2:]) * inv], axis=0)


def _filter_spectrum(a5, g, asum):
    _, _, n1s, n2, c = a5.shape
    return pl.pallas_call(
        _kf_kernel,
        grid=(n1s,),
        in_specs=[pl.BlockSpec((2, 2, 1, n2, c), lambda k: (0, 0, k, 0, 0)),
                  pl.BlockSpec((1, 2 * n2, 2 * n2), lambda k: (k, 0, 0)),
                  pl.BlockSpec((2, c), lambda k: (0, 0))],
        out_specs=pl.BlockSpec((1, 2 * n2, c), lambda k: (k, 0, 0)),
        out_shape=jax.ShapeDtypeStruct((n1s, 2 * n2, c), F32),
        compiler_params=_cp(("parallel",)),
        name="filter_spectrum",
    )(a5, g, asum)


def _spec_kernel(a_ref, g_ref, kf_ref, gi_ref, d_ref):
    n2 = a_ref.shape[3]
    c = a_ref.shape[4]
    a = a_ref[0, :, 0].reshape(2 * n2, c)
    xf = _mm(g_ref[0], a, False)
    kf = kf_ref[0]
    xr, xi, kr, ki = xf[:n2], xf[n2:], kf[:n2], kf[n2:]
    y = jnp.concatenate([xr * kr - xi * ki, xr * ki + xi * kr], axis=0)
    d = _mm(gi_ref[0], y, False)
    d_ref[0, :, 0] = d.reshape(2, n2, c).astype(d_ref.dtype)


def _spectral_multiply(a5, g, kf, gi):
    b, _, n1s, n2, c = a5.shape
    blk = pl.BlockSpec((1, 2, 1, n2, c), lambda k, bi: (bi, 0, k, 0, 0))
    mat = pl.BlockSpec((1, 2 * n2, 2 * n2), lambda k, bi: (k, 0, 0))
    return pl.pallas_call(
        _spec_kernel,
        grid=(n1s, b),
        in_specs=[blk, mat, pl.BlockSpec((1, 2 * n2, c), lambda k, bi: (k, 0, 0)), mat],
        out_specs=blk,
        out_shape=jax.ShapeDtypeStruct(a5.shape, a5.dtype),
        compiler_params=_cp(("parallel", "arbitrary")),
        name="spectral_multiply",
    )(a5, g, kf, gi)


def _inv_kernel(f_ref, d_ref, z_ref, x0_ref, bias_ref, o_ref):
    y = _mm(f_ref[...], d_ref[0], False)
    o_ref[0] = x0_ref[0] * (y + z_ref[0] * bias_ref[...])


def _dft_final(fb, d3, z3, x03, bias_t):
    b, m2, cols = d3.shape
    r = fb.shape[0]
    tn = min(4096, cols)
    rspec = pl.BlockSpec((1, r, tn), lambda bi, j: (bi, 0, j))
    return pl.pallas_call(
        _inv_kernel,
        grid=(b, cols // tn),
        in_specs=[pl.BlockSpec((r, m2), lambda bi, j: (0, 0)),
                  pl.BlockSpec((1, m2, tn), lambda bi, j: (bi, 0, j)),
                  rspec, rspec,
                  pl.BlockSpec((1, tn), lambda bi, j: (0, j))],
        out_specs=rspec,
        out_shape=jax.ShapeDtypeStruct((b, r, cols), F32),
        compiler_params=_cp(("parallel", "parallel")),
        name="dft_final",
    )(fb, d3, z3, x03, bias_t)


def _hyena_mix(z, x0, hy_bias, hf):
    b, l, c = z.shape
    n2 = DFT_N2
    n1s = 2 * l // n2
    fa, g, gi, fb = _dft_consts(l)
    hk, asum = _hyena_filter(l, *hf)
    cols = n2 * c
    ka = _dft_stage1(fa, hk.reshape(2, n1s // 2, cols), True)
    kf = _filter_spectrum(ka.reshape(2, 2, n1s, n2, c), g, asum)
    z3 = z.reshape(b, n1s // 2, cols)
    a = _dft_stage1(fa.astype(BF16), z3, False)
    d = _spectral_multiply(a.reshape(b, 2, n1s, n2, c), g.astype(BF16), kf, gi.astype(BF16))
    bias_t = jnp.tile(hy_bias.astype(F32), n2)[None, :]
    yh = _dft_final(fb.astype(BF16), d.reshape(b, 2 * n1s, cols), z3, x0.reshape(b, n1s // 2, cols), bias_t)
    return yh.reshape(b, l, c)


def _layer_norm(r, g, b):
    mu = jnp.mean(r, axis=-1, keepdims=True)
    rc = r - mu
    var = jnp.mean(rc * rc, axis=-1, keepdims=True)
    return rc * lax.rsqrt(var + LN_EPS) * g + b


def _out_proj_kernel(ya_ref, yh_ref, x_ref, wa_ref, wh_ref, g_ref, b_ref, o_ref, *, alpha):
    mix = (jnp.dot(ya_ref[...].astype(BF16), wa_ref[...], preferred_element_type=F32)
           + jnp.dot(yh_ref[...].astype(BF16), wh_ref[...], preferred_element_type=F32))
    o_ref[...] = _layer_norm(alpha * x_ref[...] + mix, g_ref[...], b_ref[...])


def _out_proj(ya, yh, x2, wa_bf, wh_bf, g, b, alpha):
    m, d = x2.shape
    cw = ya.shape[1]
    tm = min(512, m)
    row = lambda w: pl.BlockSpec((tm, w), lambda i: (i, 0))
    const = lambda shape: pl.BlockSpec(shape, lambda i: (0, 0))
    return pl.pallas_call(
        functools.partial(_out_proj_kernel, alpha=alpha),
        grid=(m // tm,),
        in_specs=[row(cw), row(cw), row(d), const((cw, d)), const((cw, d)), const((1, d)), const((1, d))],
        out_specs=row(d),
        out_shape=jax.ShapeDtypeStruct((m, d), F32),
        compiler_params=_cp(("parallel",)),
        name="out_proj_ln",
    )(ya, yh, x2, wa_bf, wh_bf, g, b)


NEG_INF = float("-inf")
NO_ROW = float(2 ** 30)


def _topk_keys(s, k):
    n, t = s.shape
    sub = lax.broadcasted_iota(jnp.int32, (SUBLANES, t), 0).astype(F32)
    slabs = [s[g * SUBLANES:(g + 1) * SUBLANES] for g in range(n // SUBLANES)]
    ids = [sub + float(g * SUBLANES) for g in range(n // SUBLANES)]
    vals, poss = [], []
    for _ in range(k):
        vs, rs = slabs, ids
        while len(vs) > 1:
            nv, nr = [], []
            for a in range(0, len(vs), 2):
                keep = vs[a] >= vs[a + 1]
                nv.append(jnp.maximum(vs[a], vs[a + 1]))
                nr.append(jnp.where(keep, rs[a], rs[a + 1]))
            vs, rs = nv, nr
        m = jnp.max(vs[0], axis=0, keepdims=True)
        pos = jnp.min(jnp.where(vs[0] == m, rs[0], NO_ROW), axis=0, keepdims=True)
        vals.append(m)
        poss.append(pos)
        slabs = [jnp.where(r == pos, NEG_INF, v) for v, r in zip(slabs, ids)]
    return jnp.concatenate(vals, axis=0), jnp.concatenate(poss, axis=0)


def _topk_ordered(s, order, payload, k):
    vals, pays = [], []
    for _ in range(k):
        m = jnp.max(s, axis=0, keepdims=True)
        first = jnp.min(jnp.where(s == m, order, NO_ROW), axis=0, keepdims=True)
        hit = order == first
        vals.append(m)
        pays.append(jnp.max(jnp.where(hit, payload, -1.0), axis=0, keepdims=True))
        s = jnp.where(hit, NEG_INF, s)
    return jnp.concatenate(vals, axis=0), jnp.concatenate(pays, axis=0)


def _pair_candidates(v1, i1, v2, i2, nkeys):
    kk, t = v1.shape
    up8 = lambda x: -(-x // SUBLANES) * SUBLANES
    lim = lambda a: kk // (a + 1)
    nfull = sum(1 for a in range(kk) if lim(a) >= SUBLANES)
    blocks = [("a", a, 0, lim(a)) for a in range(nfull)]
    blocks += [("b", b, nfull, lim(b)) for b in range(nfull)]
    blocks += [("a", a, nfull, lim(a)) for a in range(nfull, kk) if lim(a) > nfull]
    assert sum(hi - lo for _, _, lo, hi in blocks) == sum(lim(a) for a in range(kk))
    sums, poss, eids = [], [], []
    for kind, fixed, lo, hi in blocks:
        rows = up8(hi)
        it = lax.broadcasted_iota(jnp.int32, (rows, t), 0).astype(F32)
        valid = (it >= float(lo)) & (it < float(hi))
        if kind == "a":
            sm = v1[fixed:fixed + 1] + v2[:rows]
            ps = it + float(fixed * kk)
            ei = i1[fixed:fixed + 1] * float(nkeys) + i2[:rows]
        else:
            sm = v1[:rows] + v2[fixed:fixed + 1]
            ps = it * float(kk) + float(fixed)
            ei = i1[:rows] * float(nkeys) + i2[fixed:fixed + 1]
        sums.append(jnp.where(valid, sm, NEG_INF))
        poss.append(ps)
        eids.append(ei)
    cat = lambda xs: jnp.concatenate(xs, axis=0)
    return cat(sums), cat(poss), cat(eids)


def _route_kernel(x_ref, wq_ref, keys_ref, eidx_ref, gate_ref, q_scr, e_scr, g_scr, *, heads, nkeys, row_scale):
    kk = PEER_TOPK
    half = keys_ref.shape[2]
    q_scr[...] = jnp.dot(x_ref[...].astype(BF16), wq_ref[...], preferred_element_type=F32)

    def head(h, carry):
        tops = []
        for p in range(2):
            c = h * 2 + p
            qc = q_scr[:, pl.ds(pl.multiple_of(c * half, half), half)].astype(BF16)
            st = lax.dot_general(keys_ref[c], qc, (((1,), (1,)), ((), ())),
                                 preferred_element_type=F32)
            tops.append(_topk_keys(st, kk))
        (v1, i1), (v2, i2) = tops
        cand, cpos, cidx = _pair_candidates(v1, i1, v2, i2, nkeys)
        best, e = _topk_ordered(cand, cpos, cidx, kk)
        ex = jnp.exp(best - jnp.max(best, axis=0, keepdims=True))
        gate = ex / jnp.sum(ex, axis=0, keepdims=True)
        r0 = pl.multiple_of(h * kk, kk)
        e_scr[pl.ds(r0, kk), :] = e
        g_scr[pl.ds(r0, kk), :] = gate
        return carry

    lax.fori_loop(0, heads, head, 0)
    eidx_ref[...] = (e_scr[...].T * float(row_scale)).astype(jnp.int32)
    gate_ref[...] = g_scr[...].T


def _route(x2, wq_bf, keys_bf, heads, nkeys):
    m, d = x2.shape
    qd = wq_bf.shape[1]
    tt = min(256, m)
    hk = heads * PEER_TOPK
    return pl.pallas_call(
        functools.partial(_route_kernel, heads=heads, nkeys=nkeys, row_scale=PACKED_ROWS),
        grid=(m // tt,),
        in_specs=[pl.BlockSpec((tt, d), lambda i: (i, 0)),
                  pl.BlockSpec((d, qd), lambda i: (0, 0)),
                  pl.BlockSpec(keys_bf.shape, lambda i: (0, 0, 0))],
        out_specs=[pl.BlockSpec((tt, hk), lambda i: (i, 0)),
                   pl.BlockSpec((tt, hk), lambda i: (i, 0))],
        out_shape=[jax.ShapeDtypeStruct((m, hk), jnp.int32), jax.ShapeDtypeStruct((m, hk), F32)],
        scratch_shapes=[pltpu.VMEM((tt, qd), F32), pltpu.VMEM((hk, tt), F32), pltpu.VMEM((hk, tt), F32)],
        compiler_params=_cp(("parallel",)),
        name="peer_route",
    )(x2, wq_bf, keys_bf)


SUBLANES = 8
EXPERT_TOKENS = 64


def _pack_table(tab):
    e, d = tab.shape
    bits = lax.bitcast_convert_type(tab.astype(BF16).reshape(e, d // (2 * LANES), 2, LANES), jnp.uint16)
    bits = bits.astype(jnp.uint32)
    words = lax.bitcast_convert_type(bits[:, :, 0, :] | (bits[:, :, 1, :] << 16), jnp.int32)
    return words.reshape(e * (d // (2 * LANES)), LANES)


PACKED_ROWS = SUBLANES // 2


def _gather_rows(tab_ref, off_ref, t, nk):
    tiles = [tab_ref[pl.ds(pl.multiple_of(off_ref[t, k], PACKED_ROWS), PACKED_ROWS), :] for k in range(nk)]
    return pltpu.bitcast(jnp.concatenate(tiles, axis=0), BF16)


def _diag_mask(n):
    lane = lax.broadcasted_iota(jnp.int32, (SUBLANES, n), 1)
    return (lane & (SUBLANES - 1)) == lax.broadcasted_iota(jnp.int32, (SUBLANES, n), 0)


def _with_offsets(off_hbm, bufs, sems, body):
    i = pl.program_id(0)
    n = pl.num_programs(0)
    tt = bufs[0].shape[0]

    def fetch(step, slot):
        return pltpu.make_async_copy(off_hbm.at[pl.ds(step * tt, tt)], bufs[slot], sems.at[slot])

    @pl.when(i == 0)
    def _():
        fetch(0, 0).start()

    for slot in range(2):
        @pl.when(i % 2 == slot)
        def _(slot=slot):
            @pl.when(i + 1 < n)
            def _():
                fetch(i + 1, 1 - slot).start()

            fetch(i, slot).wait()
            body(bufs[slot])


def _peer_u_kernel(off_hbm, x_ref, u_ref, o_ref, off_a, off_b, sems):
    tt, nk = off_a.shape
    diag = _diag_mask(SUBLANES * nk)

    def body(off_ref):
        for t in range(tt):
            r = _gather_rows(u_ref, off_ref, t, nk)
            xt = x_ref[t].astype(BF16)
            p = lax.dot_general(xt, r, (((1,), (1,)), ((), ())), preferred_element_type=F32)
            o_ref[pl.ds(t, 1), :] = jnp.sum(jnp.where(diag, p, 0.0), axis=0, keepdims=True)

    _with_offsets(off_hbm, (off_a, off_b), sems, body)


def _peer_u(eidx, x3, u_pk):
    m, nk = eidx.shape
    ln = u_pk.shape[1]
    tt = min(EXPERT_TOKENS, m)
    n = SUBLANES * nk
    return pl.pallas_call(
        _peer_u_kernel,
        grid=(m // tt,),
        in_specs=[pl.BlockSpec(memory_space=pl.ANY),
                  pl.BlockSpec((tt, SUBLANES, ln), lambda i: (i, 0, 0)),
                  pl.BlockSpec(u_pk.shape, lambda i: (0, 0), pipeline_mode=pl.Buffered(1))],
        out_specs=pl.BlockSpec((tt, n), lambda i: (i, 0)),
        out_shape=jax.ShapeDtypeStruct((m, n), F32),
        scratch_shapes=[pltpu.SMEM((tt, nk), jnp.int32), pltpu.SMEM((tt, nk), jnp.int32),
                        pltpu.SemaphoreType.DMA((2,))],
        compiler_params=_cp(("arbitrary",)),
        name="peer_u",
    )(eidx, x3, u_pk)


def _dot_exact01(x, w):
    acc = None
    rest = x
    for _ in range(3):
        piece = rest.astype(BF16)
        rest = rest - piece.astype(F32)
        d = jnp.dot(piece, w, preferred_element_type=F32)
        acc = d if acc is None else acc + d
    return acc


def _coef_kernel(part_ref, gate_ref, o_ref):
    n = part_ref.shape[1]
    nk = gate_ref.shape[1]
    sub = n // nk
    fold = (lax.broadcasted_iota(jnp.int32, (n, nk), 0) // sub
            == lax.broadcasted_iota(jnp.int32, (n, nk), 1)).astype(BF16)
    spread = (lax.broadcasted_iota(jnp.int32, (nk, n), 1) // sub
              == lax.broadcasted_iota(jnp.int32, (nk, n), 0)).astype(BF16)
    a = _dot_exact01(part_ref[...], fold)
    gelu = 0.5 * a * (1.0 + lax.erf(a * (1.0 / math.sqrt(2.0))))
    o_ref[...] = _dot_exact01(gate_ref[...] * gelu, spread)


def _coef(part, gate):
    m, n = part.shape
    nk = gate.shape[1]
    tt = min(1024, m)
    return pl.pallas_call(
        _coef_kernel,
        grid=(m // tt,),
        in_specs=[pl.BlockSpec((tt, n), lambda i: (i, 0)),
                  pl.BlockSpec((tt, nk), lambda i: (i, 0))],
        out_specs=pl.BlockSpec((tt, n), lambda i: (i, 0)),
        out_shape=jax.ShapeDtypeStruct((m, n), F32),
        compiler_params=_cp(("parallel",)),
        name="peer_coef",
    )(part, gate)


def _peer_v_kernel(off_hbm, c_ref, v_ref, o_ref, off_a, off_b, sems):
    tt, nk = off_a.shape
    diag = _diag_mask(SUBLANES * nk)

    def body(off_ref):
        for t in range(tt):
            r = _gather_rows(v_ref, off_ref, t, nk)
            c8 = jnp.where(diag, c_ref[pl.ds(t, 1), :], 0.0).astype(BF16)
            o_ref[t] = jnp.dot(c8, r, preferred_element_type=F32)

    _with_offsets(off_hbm, (off_a, off_b), sems, body)


def _peer_v(eidx, coef8, v_pk):
    m, nk = eidx.shape
    ln = v_pk.shape[1]
    tt = min(EXPERT_TOKENS, m)
    n = SUBLANES * nk
    return pl.pallas_call(
        _peer_v_kernel,
        grid=(m // tt,),
        in_specs=[pl.BlockSpec(memory_space=pl.ANY),
                  pl.BlockSpec((tt, n), lambda i: (i, 0)),
                  pl.BlockSpec(v_pk.shape, lambda i: (0, 0), pipeline_mode=pl.Buffered(1))],
        out_specs=pl.BlockSpec((tt, SUBLANES, ln), lambda i: (i, 0, 0)),
        out_shape=jax.ShapeDtypeStruct((m, SUBLANES, ln), F32),
        scratch_shapes=[pltpu.SMEM((tt, nk), jnp.int32), pltpu.SMEM((tt, nk), jnp.int32),
                        pltpu.SemaphoreType.DMA((2,))],
        compiler_params=_cp(("arbitrary",)),
        name="peer_v",
    )(eidx, coef8, v_pk)


def _res_ln_kernel(x_ref, y_ref, g_ref, b_ref, o_ref, *, alpha):
    o_ref[...] = _layer_norm(alpha * x_ref[...] + y_ref[...], g_ref[...], b_ref[...])


def _res_ln(x2, y2, g, b, alpha):
    m, d = x2.shape
    tm = min(1024, m)
    row = pl.BlockSpec((tm, d), lambda i: (i, 0))
    const = pl.BlockSpec((1, d), lambda i: (0, 0))
    return pl.pallas_call(
        functools.partial(_res_ln_kernel, alpha=alpha),
        grid=(m // tm,),
        in_specs=[row, row, const, const],
        out_specs=row,
        out_shape=jax.ShapeDtypeStruct((m, d), F32),
        compiler_params=_cp(("parallel",)),
        name="residual_ln",
    )(x2, y2, g, b)


def _peer(x2, wq_bf, keys_bf, u_pk, v_pk, heads, nkeys):
    m, d = x2.shape
    assert d == SUBLANES * LANES, "the expert stages keep one token row per (8, 128) vreg"
    eidx, gate = _route(x2, wq_bf, keys_bf, heads, nkeys)
    part = _peer_u(eidx, x2.reshape(m, SUBLANES, LANES), u_pk)
    coef8 = _coef(part, gate)
    return _peer_v(eidx, coef8, v_pk).reshape(m, d)


def _encoder_layer(x, lw, alpha):
    b, l, d = x.shape
    cw = lw["a_conv_w"].shape[1]
    x2 = x.reshape(b * l, d)
    p = _in_proj(x2, lw["w_in"], lw["b_in"])
    ya, x0, z = _mix(p.reshape(b, l, -1), lw["a_conv_w"], lw["h_conv_w"], lw["h_conv_b"], cw)
    yh = _hyena_mix(z, x0, lw["hy_bias"], lw["hf"])
    x1 = _out_proj(ya.reshape(b * l, cw), yh.reshape(b * l, -1), x2, lw["w_out_a"], lw["w_out_h"],
                   lw["ln1_g"], lw["ln1_b"], alpha)
    y = _peer(x1, lw["peer_wq"], lw["peer_keys"], lw["peer_u"], lw["peer_v"], lw["heads"], lw["nkeys"])
    return _res_ln(x1, y, lw["ln2_g"], lw["ln2_b"], alpha).reshape(b, l, d)


def kernel(x_prompt, x_sample, w_in, b_in, a_conv_w, h_conv_w, h_conv_b, hf_w1, hf_b1, hf_freq, hf_w2,
           hf_b2, hf_w3, hy_decay, hy_bias, w_out, ln1_g, ln1_b, peer_wq, peer_keys, peer_u, peer_v,
           ln2_g, ln2_b):
    depth = w_in.shape[0]
    alpha = (2.0 * depth) ** 0.25
    layers = []
    for i in range(depth):
        cw = a_conv_w.shape[2]
        heads, _, nkeys, half = peer_keys.shape[1:]
        layers.append(dict(
            w_in=w_in[i].astype(BF16), b_in=b_in[i][None].astype(F32),
            a_conv_w=a_conv_w[i], h_conv_w=h_conv_w[i], h_conv_b=h_conv_b[i][None],
            hf=(hf_w1[i], hf_b1[i], hf_freq[i], hf_w2[i], hf_b2[i], hf_w3[i], hy_decay[i]),
            hy_bias=hy_bias[i],
            w_out_a=w_out[i, :cw].astype(BF16), w_out_h=w_out[i, cw:].astype(BF16),
            ln1_g=ln1_g[i][None], ln1_b=ln1_b[i][None],
            peer_wq=peer_wq[i].astype(BF16),
            peer_keys=peer_keys[i].reshape(heads * 2, nkeys, half).astype(BF16),
            peer_u=_pack_table(peer_u[i]), peer_v=_pack_table(peer_v[i]),
            ln2_g=ln2_g[i][None], ln2_b=ln2_b[i][None], heads=heads, nkeys=nkeys))

    def trunk(x):
        for lw in layers:
            x = _encoder_layer(x, lw, alpha)
        return x

    return trunk(x_prompt), trunk(x_sample)
```

```python
import functools
import math

import jax
import jax.numpy as jnp
from jax import lax
from jax.experimental import pallas as pl
from jax.experimental.pallas import tpu as pltpu

F32 = jnp.float32
BF16 = jnp.bfloat16
HI = lax.Precision.HIGHEST

LANES = 128
SHORT_K = 3
PEER_TOPK = 16
LN_EPS = 1e-5
DFT_N2 = 128
VMEM_LIMIT = 56 * 1024 * 1024


def _cp(sem, vmem=VMEM_LIMIT):
    return pltpu.CompilerParams(dimension_semantics=sem, vmem_limit_bytes=vmem)


def _in_proj_kernel(x_ref, w_ref, b_ref, o_ref):
    o_ref[...] = jnp.dot(x_ref[...].astype(BF16), w_ref[...],
                         preferred_element_type=F32) + b_ref[...]


def _in_proj(x2, w_bf, b):
    m, d = x2.shape
    n = w_bf.shape[1]
    tm = min(512, m)
    return pl.pallas_call(
        _in_proj_kernel,
        grid=(m // tm,),
        in_specs=[pl.BlockSpec((tm, d), lambda i: (i, 0)),
                  pl.BlockSpec((d, n), lambda i: (0, 0)),
                  pl.BlockSpec((1, n), lambda i: (0, 0))],
        out_specs=pl.BlockSpec((tm, n), lambda i: (i, 0)),
        out_shape=jax.ShapeDtypeStruct((m, n), F32),
        compiler_params=_cp(("parallel",)),
        name="in_proj",
    )(x2, w_bf, b)


def _mix_kernel(pm_ref, pp_ref, pn_ref, aw_ref, hw_ref, hb_ref, ya_ref, x0_ref, z_ref, *, cw):
    i = pl.program_id(1)
    n = pl.num_programs(1)
    tl = pm_ref.shape[1]
    rows = lax.broadcasted_iota(jnp.int32, (tl, 1), 0)
    has_prev = i > 0
    has_next = i < n - 1

    def conv3(cur, prev_row, next_row, w):
        prev_row = jnp.where(has_prev, prev_row, 0.0)
        next_row = jnp.where(has_next, next_row, 0.0)
        dn = jnp.where(rows == 0, prev_row, pltpu.roll(cur, 1, 0))
        up = jnp.where(rows == tl - 1, next_row, pltpu.roll(cur, tl - 1, 0))
        return dn * w[0:1] + cur * w[1:2] + up * w[2:3]

    def sec(ref, r0, r1, j):
        return ref[0, r0:r1, j * cw:(j + 1) * cw]

    g = sec(pm_ref, 0, tl, 1) * sec(pm_ref, 0, tl, 2)
    gp = sec(pp_ref, 7, 8, 1) * sec(pp_ref, 7, 8, 2)
    gn = sec(pn_ref, 0, 1, 1) * sec(pn_ref, 0, 1, 2)
    ya_ref[0] = sec(pm_ref, 0, tl, 0) * conv3(g, gp, gn, aw_ref[...])

    def hconv(j):
        c = 3 + j
        w = hw_ref[:, j * cw:(j + 1) * cw]
        return conv3(sec(pm_ref, 0, tl, c), sec(pp_ref, 7, 8, c), sec(pn_ref, 0, 1, c), w) \
            + hb_ref[:, j * cw:(j + 1) * cw]

    x0_ref[0] = hconv(0)
    z_ref[0] = hconv(2) * hconv(1)


def _mix(p3, a_conv_w, h_conv_w, h_conv_b, cw):
    b, l, pw = p3.shape
    tl = min(512, l)
    nl = l // tl
    r8 = tl // 8
    out = jax.ShapeDtypeStruct((b, l, cw), F32)
    ospec = pl.BlockSpec((1, tl, cw), lambda bi, i: (bi, i, 0))
    return pl.pallas_call(
        functools.partial(_mix_kernel, cw=cw),
        grid=(b, nl),
        in_specs=[pl.BlockSpec((1, tl, pw), lambda bi, i: (bi, i, 0)),
                  pl.BlockSpec((1, 8, pw), lambda bi, i: (bi, jnp.maximum(i * r8 - 1, 0), 0)),
                  pl.BlockSpec((1, 8, pw), lambda bi, i: (bi, jnp.minimum((i + 1) * r8, l // 8 - 1), 0)),
                  pl.BlockSpec((SHORT_K, cw), lambda bi, i: (0, 0)),
                  pl.BlockSpec((SHORT_K, 3 * cw), lambda bi, i: (0, 0)),
                  pl.BlockSpec((1, 3 * cw), lambda bi, i: (0, 0))],
        out_specs=[ospec, ospec, ospec],
        out_shape=[out, out, out],
        compiler_params=_cp(("parallel", "parallel")),
        name="mix",
    )(p3, p3, p3, a_conv_w, h_conv_w, h_conv_b)


def _filter_kernel(t_ref, w_ref, f_ref, w1t_ref, w1c_ref, w1s_ref, b1_ref, fr_ref, w2_ref, b2_ref,
                   w3_ref, dec_ref, hk_ref, asum_ref, *, hw):
    i = pl.program_id(0)
    tl = t_ref.shape[0]
    t = t_ref[...]
    ang = w_ref[...] * f_ref[...]
    z1 = (t * w1t_ref[...]
          + jnp.dot(jnp.cos(ang), w1c_ref[...], precision=HI, preferred_element_type=F32)
          + jnp.dot(-jnp.sin(ang), w1s_ref[...], precision=HI, preferred_element_type=F32)
          + b1_ref[...])
    fr = fr_ref[...]
    h = jnp.sin(fr * z1)
    h = jnp.sin(fr * (jnp.dot(h, w2_ref[...], precision=HI, preferred_element_type=F32) + b2_ref[...]))
    h = jnp.dot(h, w3_ref[...], precision=HI, preferred_element_type=F32)
    h = h * jnp.exp(-t * jnp.abs(dec_ref[...]))
    rows = i * tl + lax.broadcasted_iota(jnp.int32, (tl, 1), 0)
    fwd = h[:, :hw]
    bwd = jnp.where(rows == 0, 0.0, h[:, hw:])
    hk_ref[0] = fwd
    hk_ref[1] = bwd
    s = jnp.concatenate([jnp.sum(jnp.abs(fwd), axis=0, keepdims=True),
                         jnp.sum(jnp.abs(bwd), axis=0, keepdims=True)], axis=0)

    @pl.when(i == 0)
    def _():
        asum_ref[...] = jnp.zeros_like(asum_ref)

    asum_ref[...] += s


def _pad2(a, r, c):
    return jnp.zeros((r, c), F32).at[:a.shape[0], :a.shape[1]].set(a.astype(F32))


def _hyena_filter(l, hf_w1, hf_b1, hf_freq, hf_w2, hf_b2, hf_w3, hy_decay):
    bands = (hf_w1.shape[0] - 1) // 2
    hid = hf_w1.shape[1]
    hw = hy_decay.shape[1]
    t = jnp.linspace(0.0, 1.0, l, dtype=F32)[:, None]
    w = (2.0 * math.pi) * jnp.arange(l, dtype=F32)[:, None] / l
    f = jnp.linspace(1e-4, bands - 1, bands, dtype=F32)[None, :]
    p = LANES
    tl = min(512, l)
    args = (t, w, _pad2(f, 1, p), _pad2(hf_w1[0:1], 1, p), _pad2(hf_w1[1:1 + bands], p, p),
            _pad2(hf_w1[1 + bands:], p, p), _pad2(hf_b1[None], 1, p), _pad2(hf_freq[None], 1, p),
            _pad2(hf_w2, p, p), _pad2(hf_b2[None], 1, p), _pad2(hf_w3, p, 2 * hw),
            hy_decay.reshape(1, 2 * hw).astype(F32))
    del hid
    const = lambda shape: pl.BlockSpec(shape, lambda i: (0, 0))
    return pl.pallas_call(
        functools.partial(_filter_kernel, hw=hw),
        grid=(l // tl,),
        in_specs=[pl.BlockSpec((tl, 1), lambda i: (i, 0)), pl.BlockSpec((tl, 1), lambda i: (i, 0)),
                  const((1, p)), const((1, p)), const((p, p)), const((p, p)), const((1, p)),
                  const((1, p)), const((p, p)), const((1, p)), const((p, 2 * hw)), const((1, 2 * hw))],
        out_specs=[pl.BlockSpec((2, tl, hw), lambda i: (0, i, 0)),
                   pl.BlockSpec((2, hw), lambda i: (0, 0))],
        out_shape=[jax.ShapeDtypeStruct((2, l, hw), F32), jax.ShapeDtypeStruct((2, hw), F32)],
        compiler_params=_cp(("arbitrary",)),
        name="hyena_filter",
    )(*args)


def _dft_consts(l):
    n2s = DFT_N2
    n = 2 * l
    n1s = n // n2s
    two_pi = 2.0 * math.pi
    k1 = jnp.arange(n1s, dtype=jnp.int32)[:, None]
    n1 = jnp.arange(n1s // 2, dtype=jnp.int32)[None, :]
    ang = -two_pi * ((k1 * n1) % n1s).astype(F32) / n1s
    fa = jnp.concatenate([jnp.cos(ang), jnp.sin(ang)], axis=0)
    k1 = jnp.arange(n1s, dtype=jnp.int32)[:, None, None]
    k2 = jnp.arange(n2s, dtype=jnp.int32)[None, :, None]
    n2 = jnp.arange(n2s, dtype=jnp.int32)[None, None, :]
    ang = -two_pi * ((n2 * k2 * n1s + n2 * k1) % n).astype(F32) / n
    mr, mi = jnp.cos(ang), jnp.sin(ang)
    g = jnp.concatenate([jnp.concatenate([mr, -mi], 2), jnp.concatenate([mi, mr], 2)], 1)
    mrt, mit = jnp.swapaxes(mr, 1, 2), jnp.swapaxes(mi, 1, 2)
    gi = jnp.concatenate([jnp.concatenate([mrt, mit], 2), jnp.concatenate([-mit, mrt], 2)], 1)
    n1 = jnp.arange(n1s // 2, dtype=jnp.int32)[:, None]
    k1 = jnp.arange(n1s, dtype=jnp.int32)[None, :]
    ang = two_pi * ((n1 * k1) % n1s).astype(F32) / n1s
    fb = jnp.concatenate([jnp.cos(ang), -jnp.sin(ang)], axis=1) / n
    return fa, g, gi, fb


def _mm(a, b, precise):
    if precise:
        return jnp.dot(a, b, precision=HI, preferred_element_type=F32)
    return jnp.dot(a.astype(BF16), b.astype(BF16), preferred_element_type=F32)


def _lmat_kernel(f_ref, z_ref, o_ref, *, precise):
    o_ref[0] = _mm(f_ref[...], z_ref[0], precise).astype(o_ref.dtype)


def _dft_stage1(fa, z3, precise):
    b, r, cols = z3.shape
    m = fa.shape[0]
    tn = min(4096, cols)
    return pl.pallas_call(
        functools.partial(_lmat_kernel, precise=precise),
        grid=(b, cols // tn),
        in_specs=[pl.BlockSpec((m, r), lambda bi, j: (0, 0)),
                  pl.BlockSpec((1, r, tn), lambda bi, j: (bi, 0, j))],
        out_specs=pl.BlockSpec((1, m, tn), lambda bi, j: (bi, 0, j)),
        out_shape=jax.ShapeDtypeStruct((b, m, cols), F32 if precise else BF16),
        compiler_params=_cp(("parallel", "parallel")),
        name="dft_stage1",
    )(fa, z3)


def _kf_kernel(a_ref, g_ref, asum_ref, kf_ref):
    n2 = a_ref.shape[3]
    c = a_ref.shape[4]
    g = g_ref[0]
    f0 = jnp.dot(g, a_ref[0, :, 0].reshape(2 * n2, c), precision=HI, preferred_element_type=F32)
    f1 = jnp.dot(g, a_ref[1, :, 0].reshape(2 * n2, c), precision=HI, preferred_element_type=F32)
    inv = 1.0 / (asum_ref[0:1] + asum_ref[1:2])
    kf_ref[0] = jnp.concatenate([(f0[:n2] + f1[:n2]) * inv, (f0[n2:] - f1[n2:]) * inv], axis=0)


def _filter_spectrum(a5, g, asum):
    _, _, n1s, n2, c = a5.shape
    return pl.pallas_call(
        _kf_kernel,
        grid=(n1s,),
        in_specs=[pl.BlockSpec((2, 2, 1, n2, c), lambda k: (0, 0, k, 0, 0)),
                  pl.BlockSpec((1, 2 * n2, 2 * n2), lambda k: (k, 0, 0)),
                  pl.BlockSpec((2, c), lambda k: (0, 0))],
        out_specs=pl.BlockSpec((1, 2 * n2, c), lambda k: (k, 0, 0)),
        out_shape=jax.ShapeDtypeStruct((n1s, 2 * n2, c), F32),
        compiler_params=_cp(("parallel",)),
        name="filter_spectrum",
    )(a5, g, asum)


def _spec_kernel(a_ref, g_ref, kf_ref, gi_ref, d_ref):
    n2 = a_ref.shape[3]
    c = a_ref.shape[4]
    kf = kf_ref[0]
    kr, ki = kf[:n2], kf[n2:]
    for bi in range(a_ref.shape[0]):
        a = a_ref[bi, :, 0].reshape(2 * n2, c)
        xf = _mm(g_ref[0], a, False)
        xr, xi = xf[:n2], xf[n2:]
        y = jnp.concatenate([xr * kr - xi * ki, xr * ki + xi * kr], axis=0)
        d = _mm(gi_ref[0], y, False)
        d_ref[bi, :, 0] = d.reshape(2, n2, c).astype(d_ref.dtype)


def _spectral_multiply(a5, g, kf, gi):
    b, _, n1s, n2, c = a5.shape
    bb = math.gcd(b, 4)
    blk = pl.BlockSpec((bb, 2, 1, n2, c), lambda k, bi: (bi, 0, k, 0, 0))
    mat = pl.BlockSpec((1, 2 * n2, 2 * n2), lambda k, bi: (k, 0, 0))
    return pl.pallas_call(
        _spec_kernel,
        grid=(n1s, b // bb),
        in_specs=[blk, mat, pl.BlockSpec((1, 2 * n2, c), lambda k, bi: (k, 0, 0)), mat],
        out_specs=blk,
        out_shape=jax.ShapeDtypeStruct(a5.shape, a5.dtype),
        compiler_params=_cp(("parallel", "arbitrary")),
        name="spectral_multiply",
    )(a5, g, kf, gi)


def _inv_kernel(f_ref, d_ref, z_ref, x0_ref, bias_ref, o_ref):
    y = _mm(f_ref[...], d_ref[0], False)
    o_ref[0] = x0_ref[0] * (y + z_ref[0] * bias_ref[...])


def _dft_final(fb, d3, z3, x03, bias_t):
    b, m2, cols = d3.shape
    r = fb.shape[0]
    tn = min(4096, cols)
    rspec = pl.BlockSpec((1, r, tn), lambda bi, j: (bi, 0, j))
    return pl.pallas_call(
        _inv_kernel,
        grid=(b, cols // tn),
        in_specs=[pl.BlockSpec((r, m2), lambda bi, j: (0, 0)),
                  pl.BlockSpec((1, m2, tn), lambda bi, j: (bi, 0, j)),
                  rspec, rspec,
                  pl.BlockSpec((1, tn), lambda bi, j: (0, j))],
        out_specs=rspec,
        out_shape=jax.ShapeDtypeStruct((b, r, cols), F32),
        compiler_params=_cp(("parallel", "parallel")),
        name="dft_final",
    )(fb, d3, z3, x03, bias_t)


def _hyena_mix(z, x0, hy_bias, hf):
    b, l, c = z.shape
    n2 = DFT_N2
    n1s = 2 * l // n2
    fa, g, gi, fb = _dft_consts(l)
    hk, asum = _hyena_filter(l, *hf)
    cols = n2 * c
    ka = _dft_stage1(fa, hk.reshape(2, n1s // 2, cols), True)
    kf = _filter_spectrum(ka.reshape(2, 2, n1s, n2, c), g, asum)
    z3 = z.reshape(b, n1s // 2, cols)
    a = _dft_stage1(fa.astype(BF16), z3, False)
    d = _spectral_multiply(a.reshape(b, 2, n1s, n2, c), g.astype(BF16), kf, gi.astype(BF16))
    bias_t = jnp.tile(hy_bias.astype(F32), n2)[None, :]
    yh = _dft_final(fb.astype(BF16), d.reshape(b, 2 * n1s, cols), z3, x0.reshape(b, n1s // 2, cols), bias_t)
    return yh.reshape(b, l, c)


def _layer_norm(r, g, b):
    mu = jnp.mean(r, axis=-1, keepdims=True)
    rc = r - mu
    var = jnp.mean(rc * rc, axis=-1, keepdims=True)
    return rc * lax.rsqrt(var + LN_EPS) * g + b


def _out_proj_kernel(ya_ref, yh_ref, x_ref, wa_ref, wh_ref, g_ref, b_ref, o_ref, *, alpha):
    mix = (jnp.dot(ya_ref[...].astype(BF16), wa_ref[...], preferred_element_type=F32)
           + jnp.dot(yh_ref[...].astype(BF16), wh_ref[...], preferred_element_type=F32))
    o_ref[...] = _layer_norm(alpha * x_ref[...] + mix, g_ref[...], b_ref[...])


def _out_proj(ya, yh, x2, wa_bf, wh_bf, g, b, alpha):
    m, d = x2.shape
    cw = ya.shape[1]
    tm = min(512, m)
    row = lambda w: pl.BlockSpec((tm, w), lambda i: (i, 0))
    const = lambda shape: pl.BlockSpec(shape, lambda i: (0, 0))
    return pl.pallas_call(
        functools.partial(_out_proj_kernel, alpha=alpha),
        grid=(m // tm,),
        in_specs=[row(cw), row(cw), row(d), const((cw, d)), const((cw, d)), const((1, d)), const((1, d))],
        out_specs=row(d),
        out_shape=jax.ShapeDtypeStruct((m, d), F32),
        compiler_params=_cp(("parallel",)),
        name="out_proj_ln",
    )(ya, yh, x2, wa_bf, wh_bf, g, b)


NEG_INF = float("-inf")
NO_ROW = float(2 ** 30)


def _topk_keys(s, k):
    n, t = s.shape
    sub = lax.broadcasted_iota(jnp.int32, (SUBLANES, t), 0).astype(F32)
    slabs = [s[g * SUBLANES:(g + 1) * SUBLANES] for g in range(n // SUBLANES)]
    ids = [sub + float(g * SUBLANES) for g in range(n // SUBLANES)]
    vals, poss = [], []
    for _ in range(k):
        vs, rs = slabs, ids
        while len(vs) > 1:
            nv, nr = [], []
            for a in range(0, len(vs), 2):
                keep = vs[a] >= vs[a + 1]
                nv.append(jnp.maximum(vs[a], vs[a + 1]))
                nr.append(jnp.where(keep, rs[a], rs[a + 1]))
            vs, rs = nv, nr
        m = jnp.max(vs[0], axis=0, keepdims=True)
        pos = jnp.min(jnp.where(vs[0] == m, rs[0], NO_ROW), axis=0, keepdims=True)
        vals.append(m)
        poss.append(pos)
        slabs = [jnp.where(r == pos, NEG_INF, v) for v, r in zip(slabs, ids)]
    return jnp.concatenate(vals, axis=0), jnp.concatenate(poss, axis=0)


def _topk_ordered(s, order, payload, k):
    vals, pays = [], []
    for _ in range(k):
        m = jnp.max(s, axis=0, keepdims=True)
        first = jnp.min(jnp.where(s == m, order, NO_ROW), axis=0, keepdims=True)
        hit = order == first
        vals.append(m)
        pays.append(jnp.max(jnp.where(hit, payload, -1.0), axis=0, keepdims=True))
        s = jnp.where(hit, NEG_INF, s)
    return jnp.concatenate(vals, axis=0), jnp.concatenate(pays, axis=0)


def _pair_candidates(v1, i1, v2, i2, nkeys):
    kk, t = v1.shape
    up8 = lambda x: -(-x // SUBLANES) * SUBLANES
    lim = lambda a: kk // (a + 1)
    nfull = sum(1 for a in range(kk) if lim(a) >= SUBLANES)
    blocks = [("a", a, 0, lim(a)) for a in range(nfull)]
    blocks += [("b", b, nfull, lim(b)) for b in range(nfull)]
    blocks += [("a", a, nfull, lim(a)) for a in range(nfull, kk) if lim(a) > nfull]
    assert sum(hi - lo for _, _, lo, hi in blocks) == sum(lim(a) for a in range(kk))
    sums, poss, eids = [], [], []
    for kind, fixed, lo, hi in blocks:
        rows = up8(hi)
        it = lax.broadcasted_iota(jnp.int32, (rows, t), 0).astype(F32)
        valid = (it >= float(lo)) & (it < float(hi))
        if kind == "a":
            sm = v1[fixed:fixed + 1] + v2[:rows]
            ps = it + float(fixed * kk)
            ei = i1[fixed:fixed + 1] * float(nkeys) + i2[:rows]
        else:
            sm = v1[:rows] + v2[fixed:fixed + 1]
            ps = it * float(kk) + float(fixed)
            ei = i1[:rows] * float(nkeys) + i2[fixed:fixed + 1]
        sums.append(jnp.where(valid, sm, NEG_INF))
        poss.append(ps)
        eids.append(ei)
    cat = lambda xs: jnp.concatenate(xs, axis=0)
    return cat(sums), cat(poss), cat(eids)


def _route_kernel(x_ref, wq_ref, keys_ref, eidx_ref, gate_ref, q_scr, e_scr, g_scr, *, heads, nkeys, row_scale):
    kk = PEER_TOPK
    half = keys_ref.shape[2]
    q_scr[...] = jnp.dot(x_ref[...].astype(BF16), wq_ref[...], preferred_element_type=F32)

    def sub_key_tops(h):
        tops = []
        for p in range(2):
            c = h * 2 + p
            qc = q_scr[:, pl.ds(pl.multiple_of(c * half, half), half)].astype(BF16)
            st = lax.dot_general(keys_ref[c], qc, (((1,), (1,)), ((), ())),
                                 preferred_element_type=F32)
            tops.extend(_topk_keys(st, kk))
        return tuple(tops)

    def pick_experts(h, tops):
        v1, i1, v2, i2 = tops
        cand, cpos, cidx = _pair_candidates(v1, i1, v2, i2, nkeys)
        best, e = _topk_ordered(cand, cpos, cidx, kk)
        ex = jnp.exp(best - jnp.max(best, axis=0, keepdims=True))
        gate = ex / jnp.sum(ex, axis=0, keepdims=True)
        r0 = pl.multiple_of(h * kk, kk)
        e_scr[pl.ds(r0, kk), :] = e
        g_scr[pl.ds(r0, kk), :] = gate

    def step(h, tops):
        nxt = sub_key_tops(h + 1)
        pick_experts(h, tops)
        return nxt

    last = lax.fori_loop(0, heads - 1, step, sub_key_tops(0))
    pick_experts(heads - 1, last)
    eidx_ref[...] = (e_scr[...].T * float(row_scale)).astype(jnp.int32)
    gate_ref[...] = g_scr[...].T


def _route(x2, wq_bf, keys_bf, heads, nkeys):
    m, d = x2.shape
    qd = wq_bf.shape[1]
    tt = min(256, m)
    hk = heads * PEER_TOPK
    return pl.pallas_call(
        functools.partial(_route_kernel, heads=heads, nkeys=nkeys, row_scale=PACKED_ROWS),
        grid=(m // tt,),
        in_specs=[pl.BlockSpec((tt, d), lambda i: (i, 0)),
                  pl.BlockSpec((d, qd), lambda i: (0, 0)),
                  pl.BlockSpec(keys_bf.shape, lambda i: (0, 0, 0))],
        out_specs=[pl.BlockSpec((tt, hk), lambda i: (i, 0)),
                   pl.BlockSpec((tt, hk), lambda i: (i, 0))],
        out_shape=[jax.ShapeDtypeStruct((m, hk), jnp.int32), jax.ShapeDtypeStruct((m, hk), F32)],
        scratch_shapes=[pltpu.VMEM((tt, qd), F32), pltpu.VMEM((hk, tt), F32), pltpu.VMEM((hk, tt), F32)],
        compiler_params=_cp(("parallel",)),
        name="peer_route",
    )(x2, wq_bf, keys_bf)


SUBLANES = 8
EXPERT_TOKENS = 64


def _pack_table(tab):
    e, d = tab.shape
    rows = d // (2 * LANES)
    eb = math.gcd(e, 512)
    return pl.pallas_call(
        _pack_kernel,
        grid=(e // eb,),
        in_specs=[pl.BlockSpec((eb, d), lambda i: (i, 0))],
        out_specs=pl.BlockSpec((eb * rows, LANES), lambda i: (i, 0)),
        out_shape=jax.ShapeDtypeStruct((e * rows, LANES), jnp.int32),
        compiler_params=_cp(("parallel",)),
        name="pack_table",
    )(tab.astype(F32))


def _pack_kernel(t_ref, o_ref):
    eb, d = t_ref.shape
    rows = d // (2 * LANES)
    bf16_bits = lambda x: lax.bitcast_convert_type(x.astype(BF16).astype(F32), jnp.int32)
    for j in range(rows):
        lo = bf16_bits(t_ref[:, 2 * j * LANES:(2 * j + 1) * LANES])
        hi = bf16_bits(t_ref[:, (2 * j + 1) * LANES:(2 * j + 2) * LANES])
        word = (hi & jnp.int32(-65536)) | ((lo >> 16) & jnp.int32(0xFFFF))
        o_ref[pl.ds(j, eb, stride=rows), :] = word


PACKED_ROWS = SUBLANES // 2


def _gather_rows(tab_ref, off_ref, t, nk):
    tiles = [tab_ref[pl.ds(pl.multiple_of(off_ref[t, k], PACKED_ROWS), PACKED_ROWS), :] for k in range(nk)]
    return pltpu.bitcast(jnp.concatenate(tiles, axis=0), BF16)


def _diag_mask(n):
    lane = lax.broadcasted_iota(jnp.int32, (SUBLANES, n), 1)
    return (lane & (SUBLANES - 1)) == lax.broadcasted_iota(jnp.int32, (SUBLANES, n), 0)


def _with_offsets(off_hbm, bufs, sems, body):
    i = pl.program_id(0)
    n = pl.num_programs(0)
    tt = bufs[0].shape[0]

    def fetch(step, slot):
        return pltpu.make_async_copy(off_hbm.at[pl.ds(step * tt, tt)], bufs[slot], sems.at[slot])

    @pl.when(i == 0)
    def _():
        fetch(0, 0).start()

    for slot in range(2):
        @pl.when(i % 2 == slot)
        def _(slot=slot):
            @pl.when(i + 1 < n)
            def _():
                fetch(i + 1, 1 - slot).start()

            fetch(i, slot).wait()
            body(bufs[slot])


def _peer_u_kernel(off_hbm, x_ref, u_ref, o_ref, off_a, off_b, sems):
    tt, nk = off_a.shape
    diag = _diag_mask(SUBLANES * nk)

    def body(off_ref):
        for t in range(tt):
            r = _gather_rows(u_ref, off_ref, t, nk)
            xt = x_ref[pl.ds(t, 1), :].reshape(SUBLANES, LANES).astype(BF16)
            p = lax.dot_general(xt, r, (((1,), (1,)), ((), ())), preferred_element_type=F32)
            o_ref[pl.ds(t, 1), :] = jnp.sum(jnp.where(diag, p, 0.0), axis=0, keepdims=True)

    _with_offsets(off_hbm, (off_a, off_b), sems, body)


def _peer_u(eidx, x2, u_pk):
    m, nk = eidx.shape
    ln = u_pk.shape[1]
    tt = min(EXPERT_TOKENS, m)
    n = SUBLANES * nk
    return pl.pallas_call(
        _peer_u_kernel,
        grid=(m // tt,),
        in_specs=[pl.BlockSpec(memory_space=pl.ANY),
                  pl.BlockSpec((tt, SUBLANES * ln), lambda i: (i, 0)),
                  pl.BlockSpec(u_pk.shape, lambda i: (0, 0), pipeline_mode=pl.Buffered(1))],
        out_specs=pl.BlockSpec((tt, n), lambda i: (i, 0)),
        out_shape=jax.ShapeDtypeStruct((m, n), F32),
        scratch_shapes=[pltpu.SMEM((tt, nk), jnp.int32), pltpu.SMEM((tt, nk), jnp.int32),
                        pltpu.SemaphoreType.DMA((2,))],
        compiler_params=_cp(("arbitrary",)),
        name="peer_u",
    )(eidx, x2, u_pk)


def _dot_exact01(x, w):
    acc = None
    rest = x
    for _ in range(3):
        piece = rest.astype(BF16)
        rest = rest - piece.astype(F32)
        d = jnp.dot(piece, w, preferred_element_type=F32)
        acc = d if acc is None else acc + d
    return acc


def _coef_kernel(part_ref, gate_ref, o_ref):
    n = part_ref.shape[1]
    nk = gate_ref.shape[1]
    sub = n // nk
    fold = (lax.broadcasted_iota(jnp.int32, (n, nk), 0) // sub
            == lax.broadcasted_iota(jnp.int32, (n, nk), 1)).astype(BF16)
    spread = (lax.broadcasted_iota(jnp.int32, (nk, n), 1) // sub
              == lax.broadcasted_iota(jnp.int32, (nk, n), 0)).astype(BF16)
    a = _dot_exact01(part_ref[...], fold)
    gelu = 0.5 * a * (1.0 + lax.erf(a * (1.0 / math.sqrt(2.0))))
    o_ref[...] = _dot_exact01(gate_ref[...] * gelu, spread)


def _coef(part, gate):
    m, n = part.shape
    nk = gate.shape[1]
    tt = min(1024, m)
    return pl.pallas_call(
        _coef_kernel,
        grid=(m // tt,),
        in_specs=[pl.BlockSpec((tt, n), lambda i: (i, 0)),
                  pl.BlockSpec((tt, nk), lambda i: (i, 0))],
        out_specs=pl.BlockSpec((tt, n), lambda i: (i, 0)),
        out_shape=jax.ShapeDtypeStruct((m, n), F32),
        compiler_params=_cp(("parallel",)),
        name="peer_coef",
    )(part, gate)


def _peer_v_kernel(off_hbm, c_ref, v_ref, o_ref, off_a, off_b, sems):
    tt, nk = off_a.shape
    diag = _diag_mask(SUBLANES * nk)

    def body(off_ref):
        for t in range(tt):
            r = _gather_rows(v_ref, off_ref, t, nk)
            c8 = jnp.where(diag, c_ref[pl.ds(t, 1), :], 0.0).astype(BF16)
            y = jnp.dot(c8, r, preferred_element_type=F32)
            o_ref[pl.ds(t, 1), :] = y.reshape(1, SUBLANES * LANES)

    _with_offsets(off_hbm, (off_a, off_b), sems, body)


def _peer_v(eidx, coef8, v_pk):
    m, nk = eidx.shape
    ln = v_pk.shape[1]
    tt = min(EXPERT_TOKENS, m)
    n = SUBLANES * nk
    return pl.pallas_call(
        _peer_v_kernel,
        grid=(m // tt,),
        in_specs=[pl.BlockSpec(memory_space=pl.ANY),
                  pl.BlockSpec((tt, n), lambda i: (i, 0)),
                  pl.BlockSpec(v_pk.shape, lambda i: (0, 0), pipeline_mode=pl.Buffered(1))],
        out_specs=pl.BlockSpec((tt, SUBLANES * ln), lambda i: (i, 0)),
        out_shape=jax.ShapeDtypeStruct((m, SUBLANES * ln), F32),
        scratch_shapes=[pltpu.SMEM((tt, nk), jnp.int32), pltpu.SMEM((tt, nk), jnp.int32),
                        pltpu.SemaphoreType.DMA((2,))],
        compiler_params=_cp(("arbitrary",)),
        name="peer_v",
    )(eidx, coef8, v_pk)


def _res_ln_kernel(x_ref, y_ref, g_ref, b_ref, o_ref, *, alpha):
    o_ref[...] = _layer_norm(alpha * x_ref[...] + y_ref[...], g_ref[...], b_ref[...])


def _res_ln(x2, y2, g, b, alpha):
    m, d = x2.shape
    tm = min(1024, m)
    row = pl.BlockSpec((tm, d), lambda i: (i, 0))
    const = pl.BlockSpec((1, d), lambda i: (0, 0))
    return pl.pallas_call(
        functools.partial(_res_ln_kernel, alpha=alpha),
        grid=(m // tm,),
        in_specs=[row, row, const, const],
        out_specs=row,
        out_shape=jax.ShapeDtypeStruct((m, d), F32),
        compiler_params=_cp(("parallel",)),
        name="residual_ln",
    )(x2, y2, g, b)


def _peer(x2, wq_bf, keys_bf, u_pk, v_pk, heads, nkeys):
    m, d = x2.shape
    assert d == SUBLANES * LANES, "the expert stages keep one token row per (8, 128) vreg"
    eidx, gate = _route(x2, wq_bf, keys_bf, heads, nkeys)
    part = _peer_u(eidx, x2, u_pk)
    coef8 = _coef(part, gate)
    return _peer_v(eidx, coef8, v_pk)


def _encoder_layer(x, lw, alpha):
    b, l, d = x.shape
    cw = lw["a_conv_w"].shape[1]
    x2 = x.reshape(b * l, d)
    p = _in_proj(x2, lw["w_in"], lw["b_in"])
    ya, x0, z = _mix(p.reshape(b, l, -1), lw["a_conv_w"], lw["h_conv_w"], lw["h_conv_b"], cw)
    yh = _hyena_mix(z, x0, lw["hy_bias"], lw["hf"])
    x1 = _out_proj(ya.reshape(b * l, cw), yh.reshape(b * l, -1), x2, lw["w_out_a"], lw["w_out_h"],
                   lw["ln1_g"], lw["ln1_b"], alpha)
    y = _peer(x1, lw["peer_wq"], lw["peer_keys"], lw["peer_u"], lw["peer_v"], lw["heads"], lw["nkeys"])
    return _res_ln(x1, y, lw["ln2_g"], lw["ln2_b"], alpha).reshape(b, l, d)


def kernel(x_prompt, x_sample, w_in, b_in, a_conv_w, h_conv_w, h_conv_b, hf_w1, hf_b1, hf_freq, hf_w2,
           hf_b2, hf_w3, hy_decay, hy_bias, w_out, ln1_g, ln1_b, peer_wq, peer_keys, peer_u, peer_v,
           ln2_g, ln2_b):
    depth = w_in.shape[0]
    alpha = (2.0 * depth) ** 0.25
    layers = []
    for i in range(depth):
        cw = a_conv_w.shape[2]
        heads, _, nkeys, half = peer_keys.shape[1:]
        layers.append(dict(
            w_in=w_in[i].astype(BF16), b_in=b_in[i][None].astype(F32),
            a_conv_w=a_conv_w[i], h_conv_w=h_conv_w[i], h_conv_b=h_conv_b[i][None],
            hf=(hf_w1[i], hf_b1[i], hf_freq[i], hf_w2[i], hf_b2[i], hf_w3[i], hy_decay[i]),
            hy_bias=hy_bias[i],
            w_out_a=w_out[i, :cw].astype(BF16), w_out_h=w_out[i, cw:].astype(BF16),
            ln1_g=ln1_g[i][None], ln1_b=ln1_b[i][None],
            peer_wq=peer_wq[i].astype(BF16),
            peer_keys=peer_keys[i].reshape(heads * 2, nkeys, half).astype(BF16),
            peer_u=_pack_table(peer_u[i]), peer_v=_pack_table(peer_v[i]),
            ln2_g=ln2_g[i][None], ln2_b=ln2_b[i][None], heads=heads, nkeys=nkeys))

    def trunk(x):
        for lw in layers:
            x = _encoder_layer(x, lw, alpha)
        return x

    return trunk(x_prompt), trunk(x_sample)
```

```python
import functools
import math

import jax
import jax.numpy as jnp
from jax import lax
from jax.experimental import pallas as pl
from jax.experimental.pallas import tpu as pltpu

F32 = jnp.float32
BF16 = jnp.bfloat16
HI = lax.Precision.HIGHEST

LANES = 128
SUBLANES = 8
PACKED_ROWS = SUBLANES // 2
SHORT_K = 3
PEER_TOPK = 16
LN_EPS = 1e-5
DFT_N2 = 128

ROW_TILE = 512
LN_ROW_TILE = 1024
DFT_COL_TILE = 4096
SPECTRAL_BATCH = 4
ROUTE_TOKENS = 256
PACK_EXPERTS = 512
EXPERT_TOKENS = 64
VMEM_LIMIT = 56 * 1024 * 1024


def _cp(sem, vmem=VMEM_LIMIT):
    return pltpu.CompilerParams(dimension_semantics=sem, vmem_limit_bytes=vmem)


def _in_proj_kernel(x_ref, w_ref, b_ref, o_ref):
    o_ref[...] = jnp.dot(x_ref[...].astype(BF16), w_ref[...],
                         preferred_element_type=F32) + b_ref[...]


def _in_proj(x2, w_bf, b):
    m, d = x2.shape
    n = w_bf.shape[1]
    tm = min(ROW_TILE, m)
    return pl.pallas_call(
        _in_proj_kernel,
        grid=(m // tm,),
        in_specs=[pl.BlockSpec((tm, d), lambda i: (i, 0)),
                  pl.BlockSpec((d, n), lambda i: (0, 0)),
                  pl.BlockSpec((1, n), lambda i: (0, 0))],
        out_specs=pl.BlockSpec((tm, n), lambda i: (i, 0)),
        out_shape=jax.ShapeDtypeStruct((m, n), F32),
        compiler_params=_cp(("parallel",)),
        name="in_proj",
    )(x2, w_bf, b)


def _mix_kernel(pm_ref, pp_ref, pn_ref, aw_ref, hw_ref, hb_ref, ya_ref, x0_ref, z_ref, *, cw):
    i = pl.program_id(1)
    n = pl.num_programs(1)
    tl = pm_ref.shape[1]
    rows = lax.broadcasted_iota(jnp.int32, (tl, 1), 0)
    has_prev = i > 0
    has_next = i < n - 1

    def conv3(cur, prev_row, next_row, w):
        prev_row = jnp.where(has_prev, prev_row, 0.0)
        next_row = jnp.where(has_next, next_row, 0.0)
        dn = jnp.where(rows == 0, prev_row, pltpu.roll(cur, 1, 0))
        up = jnp.where(rows == tl - 1, next_row, pltpu.roll(cur, tl - 1, 0))
        return dn * w[0:1] + cur * w[1:2] + up * w[2:3]

    def sec(ref, r0, r1, j):
        return ref[0, r0:r1, j * cw:(j + 1) * cw]

    g = sec(pm_ref, 0, tl, 1) * sec(pm_ref, 0, tl, 2)
    gp = sec(pp_ref, 7, 8, 1) * sec(pp_ref, 7, 8, 2)
    gn = sec(pn_ref, 0, 1, 1) * sec(pn_ref, 0, 1, 2)
    ya_ref[0] = sec(pm_ref, 0, tl, 0) * conv3(g, gp, gn, aw_ref[...])

    def hconv(j):
        c = 3 + j
        w = hw_ref[:, j * cw:(j + 1) * cw]
        return conv3(sec(pm_ref, 0, tl, c), sec(pp_ref, 7, 8, c), sec(pn_ref, 0, 1, c), w) \
            + hb_ref[:, j * cw:(j + 1) * cw]

    x0_ref[0] = hconv(0)
    z_ref[0] = hconv(2) * hconv(1)


def _mix(p3, a_conv_w, h_conv_w, h_conv_b, cw):
    b, l, pw = p3.shape
    tl = min(ROW_TILE, l)
    nl = l // tl
    r8 = tl // 8
    out = jax.ShapeDtypeStruct((b, l, cw), F32)
    ospec = pl.BlockSpec((1, tl, cw), lambda bi, i: (bi, i, 0))
    return pl.pallas_call(
        functools.partial(_mix_kernel, cw=cw),
        grid=(b, nl),
        in_specs=[pl.BlockSpec((1, tl, pw), lambda bi, i: (bi, i, 0)),
                  pl.BlockSpec((1, 8, pw), lambda bi, i: (bi, jnp.maximum(i * r8 - 1, 0), 0)),
                  pl.BlockSpec((1, 8, pw), lambda bi, i: (bi, jnp.minimum((i + 1) * r8, l // 8 - 1), 0)),
                  pl.BlockSpec((SHORT_K, cw), lambda bi, i: (0, 0)),
                  pl.BlockSpec((SHORT_K, 3 * cw), lambda bi, i: (0, 0)),
                  pl.BlockSpec((1, 3 * cw), lambda bi, i: (0, 0))],
        out_specs=[ospec, ospec, ospec],
        out_shape=[out, out, out],
        compiler_params=_cp(("parallel", "parallel")),
        name="mix",
    )(p3, p3, p3, a_conv_w, h_conv_w, h_conv_b)


def _filter_kernel(t_ref, w_ref, f_ref, w1t_ref, w1c_ref, w1s_ref, b1_ref, fr_ref, w2_ref, b2_ref,
                   w3_ref, dec_ref, hk_ref, asum_ref, *, hw):
    i = pl.program_id(0)
    tl = t_ref.shape[0]
    t = t_ref[...]
    ang = w_ref[...] * f_ref[...]
    z1 = (t * w1t_ref[...]
          + jnp.dot(jnp.cos(ang), w1c_ref[...], precision=HI, preferred_element_type=F32)
          + jnp.dot(-jnp.sin(ang), w1s_ref[...], precision=HI, preferred_element_type=F32)
          + b1_ref[...])
    fr = fr_ref[...]
    h = jnp.sin(fr * z1)
    h = jnp.sin(fr * (jnp.dot(h, w2_ref[...], precision=HI, preferred_element_type=F32) + b2_ref[...]))
    h = jnp.dot(h, w3_ref[...], precision=HI, preferred_element_type=F32)
    h = h * jnp.exp(-t * jnp.abs(dec_ref[...]))
    rows = i * tl + lax.broadcasted_iota(jnp.int32, (tl, 1), 0)
    fwd = h[:, :hw]
    bwd = jnp.where(rows == 0, 0.0, h[:, hw:])
    hk_ref[0] = fwd
    hk_ref[1] = bwd
    s = jnp.concatenate([jnp.sum(jnp.abs(fwd), axis=0, keepdims=True),
                         jnp.sum(jnp.abs(bwd), axis=0, keepdims=True)], axis=0)

    @pl.when(i == 0)
    def _():
        asum_ref[...] = jnp.zeros_like(asum_ref)

    asum_ref[...] += s


def _pad2(a, r, c):
    return jnp.zeros((r, c), F32).at[:a.shape[0], :a.shape[1]].set(a.astype(F32))


def _hyena_filter(l, hf_w1, hf_b1, hf_freq, hf_w2, hf_b2, hf_w3, hy_decay):
    bands = (hf_w1.shape[0] - 1) // 2
    assert bands <= LANES and hf_w1.shape[1] <= LANES, "filter features and hidden width are padded to one lane tile"
    hw = hy_decay.shape[1]
    t = jnp.linspace(0.0, 1.0, l, dtype=F32)[:, None]
    w = (2.0 * math.pi) * jnp.arange(l, dtype=F32)[:, None] / l
    f = jnp.linspace(1e-4, bands - 1, bands, dtype=F32)[None, :]
    p = LANES
    tl = min(ROW_TILE, l)
    args = (t, w, _pad2(f, 1, p), _pad2(hf_w1[0:1], 1, p), _pad2(hf_w1[1:1 + bands], p, p),
            _pad2(hf_w1[1 + bands:], p, p), _pad2(hf_b1[None], 1, p), _pad2(hf_freq[None], 1, p),
            _pad2(hf_w2, p, p), _pad2(hf_b2[None], 1, p), _pad2(hf_w3, p, 2 * hw),
            hy_decay.reshape(1, 2 * hw).astype(F32))
    const = lambda shape: pl.BlockSpec(shape, lambda i: (0, 0))
    return pl.pallas_call(
        functools.partial(_filter_kernel, hw=hw),
        grid=(l // tl,),
        in_specs=[pl.BlockSpec((tl, 1), lambda i: (i, 0)), pl.BlockSpec((tl, 1), lambda i: (i, 0)),
                  const((1, p)), const((1, p)), const((p, p)), const((p, p)), const((1, p)),
                  const((1, p)), const((p, p)), const((1, p)), const((p, 2 * hw)), const((1, 2 * hw))],
        out_specs=[pl.BlockSpec((2, tl, hw), lambda i: (0, i, 0)),
                   pl.BlockSpec((2, hw), lambda i: (0, 0))],
        out_shape=[jax.ShapeDtypeStruct((2, l, hw), F32), jax.ShapeDtypeStruct((2, hw), F32)],
        compiler_params=_cp(("arbitrary",)),
        name="hyena_filter",
    )(*args)


def _dft_consts(l):
    n2s = DFT_N2
    n = 2 * l
    n1s = n // n2s
    two_pi = 2.0 * math.pi
    k1 = jnp.arange(n1s, dtype=jnp.int32)[:, None]
    n1 = jnp.arange(n1s // 2, dtype=jnp.int32)[None, :]
    ang = -two_pi * ((k1 * n1) % n1s).astype(F32) / n1s
    fa = jnp.concatenate([jnp.cos(ang), jnp.sin(ang)], axis=0)
    k1 = jnp.arange(n1s, dtype=jnp.int32)[:, None, None]
    k2 = jnp.arange(n2s, dtype=jnp.int32)[None, :, None]
    n2 = jnp.arange(n2s, dtype=jnp.int32)[None, None, :]
    ang = -two_pi * ((n2 * k2 * n1s + n2 * k1) % n).astype(F32) / n
    mr, mi = jnp.cos(ang), jnp.sin(ang)
    g = jnp.concatenate([jnp.concatenate([mr, -mi], 2), jnp.concatenate([mi, mr], 2)], 1)
    mrt, mit = jnp.swapaxes(mr, 1, 2), jnp.swapaxes(mi, 1, 2)
    gi = jnp.concatenate([jnp.concatenate([mrt, mit], 2), jnp.concatenate([-mit, mrt], 2)], 1)
    n1 = jnp.arange(n1s // 2, dtype=jnp.int32)[:, None]
    k1 = jnp.arange(n1s, dtype=jnp.int32)[None, :]
    ang = two_pi * ((n1 * k1) % n1s).astype(F32) / n1s
    fb = jnp.concatenate([jnp.cos(ang), -jnp.sin(ang)], axis=1) / n
    return fa, g, gi, fb


def _mm(a, b, precise):
    dot = lambda x, y: jnp.dot(x, y, preferred_element_type=F32)
    a_hi, b_hi = a.astype(BF16), b.astype(BF16)
    if not precise:
        return dot(a_hi, b_hi)
    a_lo = (a - a_hi.astype(F32)).astype(BF16)
    b_lo = (b - b_hi.astype(F32)).astype(BF16)
    return dot(a_hi, b_hi) + (dot(a_hi, b_lo) + dot(a_lo, b_hi))


def _lmat_kernel(f_ref, z_ref, o_ref, *, precise):
    o_ref[0] = _mm(f_ref[...], z_ref[0], precise).astype(o_ref.dtype)


def _dft_stage1(fa, z3, precise):
    b, r, cols = z3.shape
    m = fa.shape[0]
    tn = min(DFT_COL_TILE, cols)
    return pl.pallas_call(
        functools.partial(_lmat_kernel, precise=precise),
        grid=(b, cols // tn),
        in_specs=[pl.BlockSpec((m, r), lambda bi, j: (0, 0)),
                  pl.BlockSpec((1, r, tn), lambda bi, j: (bi, 0, j))],
        out_specs=pl.BlockSpec((1, m, tn), lambda bi, j: (bi, 0, j)),
        out_shape=jax.ShapeDtypeStruct((b, m, cols), F32 if precise else BF16),
        compiler_params=_cp(("parallel", "parallel")),
        name="dft_stage1",
    )(fa, z3)


def _kf_kernel(a_ref, g_ref, asum_ref, kf_ref):
    n2 = a_ref.shape[3]
    c = a_ref.shape[4]
    g = g_ref[0]
    f0 = _mm(g, a_ref[0, :, 0].reshape(2 * n2, c), True)
    f1 = _mm(g, a_ref[1, :, 0].reshape(2 * n2, c), True)
    inv = 1.0 / (asum_ref[0:1] + asum_ref[1:2])
    kf_ref[0] = jnp.concatenate([(f0[:n2] + f1[:n2]) * inv, (f0[n2:] - f1[n2:]) * inv], axis=0)


def _filter_spectrum(a5, g, asum):
    _, _, n1s, n2, c = a5.shape
    return pl.pallas_call(
        _kf_kernel,
        grid=(n1s,),
        in_specs=[pl.BlockSpec((2, 2, 1, n2, c), lambda k: (0, 0, k, 0, 0)),
                  pl.BlockSpec((1, 2 * n2, 2 * n2), lambda k: (k, 0, 0)),
                  pl.BlockSpec((2, c), lambda k: (0, 0))],
        out_specs=pl.BlockSpec((1, 2 * n2, c), lambda k: (k, 0, 0)),
        out_shape=jax.ShapeDtypeStruct((n1s, 2 * n2, c), F32),
        compiler_params=_cp(("parallel",)),
        name="filter_spectrum",
    )(a5, g, asum)


def _spec_kernel(a_ref, g_ref, kf_ref, gi_ref, d_ref):
    n2 = a_ref.shape[3]
    c = a_ref.shape[4]
    kf = kf_ref[0]
    kr, ki = kf[:n2], kf[n2:]
    for bi in range(a_ref.shape[0]):
        a = a_ref[bi, :, 0].reshape(2 * n2, c)
        xf = _mm(g_ref[0], a, False)
        xr, xi = xf[:n2], xf[n2:]
        y = jnp.concatenate([xr * kr - xi * ki, xr * ki + xi * kr], axis=0)
        d = _mm(gi_ref[0], y, False)
        d_ref[bi, :, 0] = d.reshape(2, n2, c).astype(d_ref.dtype)


def _spectral_multiply(a5, g, kf, gi):
    b, _, n1s, n2, c = a5.shape
    bb = math.gcd(b, SPECTRAL_BATCH)
    blk = pl.BlockSpec((bb, 2, 1, n2, c), lambda k, bi: (bi, 0, k, 0, 0))
    mat = pl.BlockSpec((1, 2 * n2, 2 * n2), lambda k, bi: (k, 0, 0))
    return pl.pallas_call(
        _spec_kernel,
        grid=(n1s, b // bb),
        in_specs=[blk, mat, pl.BlockSpec((1, 2 * n2, c), lambda k, bi: (k, 0, 0)), mat],
        out_specs=blk,
        out_shape=jax.ShapeDtypeStruct(a5.shape, a5.dtype),
        compiler_params=_cp(("parallel", "arbitrary")),
        name="spectral_multiply",
    )(a5, g, kf, gi)


def _inv_kernel(f_ref, d_ref, z_ref, x0_ref, bias_ref, o_ref):
    y = _mm(f_ref[...], d_ref[0], False)
    o_ref[0] = x0_ref[0] * (y + z_ref[0] * bias_ref[...])


def _dft_final(fb, d3, z3, x03, bias_t):
    b, m2, cols = d3.shape
    r = fb.shape[0]
    tn = min(DFT_COL_TILE, cols)
    rspec = pl.BlockSpec((1, r, tn), lambda bi, j: (bi, 0, j))
    return pl.pallas_call(
        _inv_kernel,
        grid=(b, cols // tn),
        in_specs=[pl.BlockSpec((r, m2), lambda bi, j: (0, 0)),
                  pl.BlockSpec((1, m2, tn), lambda bi, j: (bi, 0, j)),
                  rspec, rspec,
                  pl.BlockSpec((1, tn), lambda bi, j: (0, j))],
        out_specs=rspec,
        out_shape=jax.ShapeDtypeStruct((b, r, cols), F32),
        compiler_params=_cp(("parallel", "parallel")),
        name="dft_final",
    )(fb, d3, z3, x03, bias_t)


def _hyena_mix(z, x0, hy_bias, hf):
    b, l, c = z.shape
    n2 = DFT_N2
    n1s = 2 * l // n2
    fa, g, gi, fb = _dft_consts(l)
    hk, asum = _hyena_filter(l, *hf)
    cols = n2 * c
    ka = _dft_stage1(fa, hk.reshape(2, n1s // 2, cols), True)
    kf = _filter_spectrum(ka.reshape(2, 2, n1s, n2, c), g, asum)
    z3 = z.reshape(b, n1s // 2, cols)
    a = _dft_stage1(fa.astype(BF16), z3, False)
    d = _spectral_multiply(a.reshape(b, 2, n1s, n2, c), g.astype(BF16), kf, gi.astype(BF16))
    bias_t = jnp.tile(hy_bias.astype(F32), n2)[None, :]
    yh = _dft_final(fb.astype(BF16), d.reshape(b, 2 * n1s, cols), z3, x0.reshape(b, n1s // 2, cols), bias_t)
    return yh.reshape(b, l, c)


def _layer_norm(r, g, b):
    mu = jnp.mean(r, axis=-1, keepdims=True)
    rc = r - mu
    var = jnp.mean(rc * rc, axis=-1, keepdims=True)
    return rc * lax.rsqrt(var + LN_EPS) * g + b


def _out_proj_kernel(ya_ref, yh_ref, x_ref, wa_ref, wh_ref, g_ref, b_ref, o_ref, *, alpha):
    mix = (jnp.dot(ya_ref[...].astype(BF16), wa_ref[...], preferred_element_type=F32)
           + jnp.dot(yh_ref[...].astype(BF16), wh_ref[...], preferred_element_type=F32))
    o_ref[...] = _layer_norm(alpha * x_ref[...] + mix, g_ref[...], b_ref[...])


def _out_proj(ya, yh, x2, wa_bf, wh_bf, g, b, alpha):
    m, d = x2.shape
    cw = ya.shape[1]
    tm = min(ROW_TILE, m)
    row = lambda w: pl.BlockSpec((tm, w), lambda i: (i, 0))
    const = lambda shape: pl.BlockSpec(shape, lambda i: (0, 0))
    return pl.pallas_call(
        functools.partial(_out_proj_kernel, alpha=alpha),
        grid=(m // tm,),
        in_specs=[row(cw), row(cw), row(d), const((cw, d)), const((cw, d)), const((1, d)), const((1, d))],
        out_specs=row(d),
        out_shape=jax.ShapeDtypeStruct((m, d), F32),
        compiler_params=_cp(("parallel",)),
        name="out_proj_ln",
    )(ya, yh, x2, wa_bf, wh_bf, g, b)


NEG_INF = float("-inf")
NO_ROW = float(2 ** 30)


def _topk_keys(s, k):
    n, t = s.shape
    sub = lax.broadcasted_iota(jnp.int32, (SUBLANES, t), 0).astype(F32)
    slabs = [s[g * SUBLANES:(g + 1) * SUBLANES] for g in range(n // SUBLANES)]
    ids = [sub + float(g * SUBLANES) for g in range(n // SUBLANES)]
    vals, poss = [], []
    for _ in range(k):
        vs, rs = slabs, ids
        while len(vs) > 1:
            nv, nr = [], []
            for a in range(0, len(vs), 2):
                keep = vs[a] >= vs[a + 1]
                nv.append(jnp.maximum(vs[a], vs[a + 1]))
                nr.append(jnp.where(keep, rs[a], rs[a + 1]))
            vs, rs = nv, nr
        m = jnp.max(vs[0], axis=0, keepdims=True)
        pos = jnp.min(jnp.where(vs[0] == m, rs[0], NO_ROW), axis=0, keepdims=True)
        vals.append(m)
        poss.append(pos)
        slabs = [jnp.where(r == pos, NEG_INF, v) for v, r in zip(slabs, ids)]
    return jnp.concatenate(vals, axis=0), jnp.concatenate(poss, axis=0)


def _topk_ordered(s, order, payload, k):
    vals, pays = [], []
    for _ in range(k):
        m = jnp.max(s, axis=0, keepdims=True)
        first = jnp.min(jnp.where(s == m, order, NO_ROW), axis=0, keepdims=True)
        hit = order == first
        vals.append(m)
        pays.append(jnp.max(jnp.where(hit, payload, -1.0), axis=0, keepdims=True))
        s = jnp.where(hit, NEG_INF, s)
    return jnp.concatenate(vals, axis=0), jnp.concatenate(pays, axis=0)


def _pair_candidates(v1, i1, v2, i2, nkeys):
    kk, t = v1.shape
    up8 = lambda x: -(-x // SUBLANES) * SUBLANES
    lim = lambda a: kk // (a + 1)
    nfull = sum(1 for a in range(kk) if lim(a) >= SUBLANES)
    blocks = [("a", a, 0, lim(a)) for a in range(nfull)]
    blocks += [("b", b, nfull, lim(b)) for b in range(nfull)]
    blocks += [("a", a, nfull, lim(a)) for a in range(nfull, kk) if lim(a) > nfull]
    assert sum(hi - lo for _, _, lo, hi in blocks) == sum(lim(a) for a in range(kk))
    sums, poss, eids = [], [], []
    for kind, fixed, lo, hi in blocks:
        rows = up8(hi)
        it = lax.broadcasted_iota(jnp.int32, (rows, t), 0).astype(F32)
        valid = (it >= float(lo)) & (it < float(hi))
        if kind == "a":
            sm = v1[fixed:fixed + 1] + v2[:rows]
            ps = it + float(fixed * kk)
            ei = i1[fixed:fixed + 1] * float(nkeys) + i2[:rows]
        else:
            sm = v1[:rows] + v2[fixed:fixed + 1]
            ps = it * float(kk) + float(fixed)
            ei = i1[:rows] * float(nkeys) + i2[fixed:fixed + 1]
        sums.append(jnp.where(valid, sm, NEG_INF))
        poss.append(ps)
        eids.append(ei)
    cat = lambda xs: jnp.concatenate(xs, axis=0)
    return cat(sums), cat(poss), cat(eids)


def _route_kernel(x_ref, wq_ref, keys_ref, eidx_ref, gate_ref, q_scr, e_scr, g_scr, *, heads, nkeys, row_scale):
    kk = PEER_TOPK
    half = keys_ref.shape[2]
    q_scr[...] = jnp.dot(x_ref[...].astype(BF16), wq_ref[...], preferred_element_type=F32)

    def sub_key_tops(h):
        tops = []
        for p in range(2):
            c = h * 2 + p
            qc = q_scr[:, pl.ds(pl.multiple_of(c * half, half), half)].astype(BF16)
            st = lax.dot_general(keys_ref[c], qc, (((1,), (1,)), ((), ())),
                                 preferred_element_type=F32)
            tops.extend(_topk_keys(st, kk))
        return tuple(tops)

    def pick_experts(h, tops):
        v1, i1, v2, i2 = tops
        cand, cpos, cidx = _pair_candidates(v1, i1, v2, i2, nkeys)
        best, e = _topk_ordered(cand, cpos, cidx, kk)
        ex = jnp.exp(best - jnp.max(best, axis=0, keepdims=True))
        gate = ex / jnp.sum(ex, axis=0, keepdims=True)
        r0 = pl.multiple_of(h * kk, kk)
        e_scr[pl.ds(r0, kk), :] = e
        g_scr[pl.ds(r0, kk), :] = gate

    def step(h, tops):
        nxt = sub_key_tops(h + 1)
        pick_experts(h, tops)
        return nxt

    last = lax.fori_loop(0, heads - 1, step, sub_key_tops(0))
    pick_experts(heads - 1, last)
    eidx_ref[...] = (e_scr[...].T * float(row_scale)).astype(jnp.int32)
    gate_ref[...] = g_scr[...].T


def _route(x2, wq_bf, keys_bf, heads, nkeys):
    m, d = x2.shape
    qd = wq_bf.shape[1]
    tt = min(ROUTE_TOKENS, m)
    hk = heads * PEER_TOPK
    return pl.pallas_call(
        functools.partial(_route_kernel, heads=heads, nkeys=nkeys, row_scale=PACKED_ROWS),
        grid=(m // tt,),
        in_specs=[pl.BlockSpec((tt, d), lambda i: (i, 0)),
                  pl.BlockSpec((d, qd), lambda i: (0, 0)),
                  pl.BlockSpec(keys_bf.shape, lambda i: (0, 0, 0))],
        out_specs=[pl.BlockSpec((tt, hk), lambda i: (i, 0)),
                   pl.BlockSpec((tt, hk), lambda i: (i, 0))],
        out_shape=[jax.ShapeDtypeStruct((m, hk), jnp.int32), jax.ShapeDtypeStruct((m, hk), F32)],
        scratch_shapes=[pltpu.VMEM((tt, qd), F32), pltpu.VMEM((hk, tt), F32), pltpu.VMEM((hk, tt), F32)],
        compiler_params=_cp(("parallel",)),
        name="peer_route",
    )(x2, wq_bf, keys_bf)


def _pack_table(tab):
    e, d = tab.shape
    rows = d // (2 * LANES)
    eb = math.gcd(e, PACK_EXPERTS)
    return pl.pallas_call(
        _pack_kernel,
        grid=(e // eb,),
        in_specs=[pl.BlockSpec((eb, d), lambda i: (i, 0))],
        out_specs=pl.BlockSpec((eb * rows, LANES), lambda i: (i, 0)),
        out_shape=jax.ShapeDtypeStruct((e * rows, LANES), jnp.int32),
        compiler_params=_cp(("parallel",)),
        name="pack_table",
    )(tab.astype(F32))


def _pack_kernel(t_ref, o_ref):
    eb, d = t_ref.shape
    rows = d // (2 * LANES)
    bf16_bits = lambda x: lax.bitcast_convert_type(x.astype(BF16).astype(F32), jnp.int32)
    for j in range(rows):
        lo = bf16_bits(t_ref[:, 2 * j * LANES:(2 * j + 1) * LANES])
        hi = bf16_bits(t_ref[:, (2 * j + 1) * LANES:(2 * j + 2) * LANES])
        word = (hi & jnp.int32(-65536)) | ((lo >> 16) & jnp.int32(0xFFFF))
        o_ref[pl.ds(j, eb, stride=rows), :] = word


def _gather_rows(tab_ref, off_ref, t, nk):
    tiles = [tab_ref[pl.ds(pl.multiple_of(off_ref[t, k], PACKED_ROWS), PACKED_ROWS), :] for k in range(nk)]
    return pltpu.bitcast(jnp.concatenate(tiles, axis=0), BF16)


def _diag_mask(n):
    lane = lax.broadcasted_iota(jnp.int32, (SUBLANES, n), 1)
    return (lane & (SUBLANES - 1)) == lax.broadcasted_iota(jnp.int32, (SUBLANES, n), 0)


def _with_offsets(off_hbm, bufs, sems, body):
    i = pl.program_id(0)
    n = pl.num_programs(0)
    tt = bufs[0].shape[0]

    def fetch(step, slot):
        return pltpu.make_async_copy(off_hbm.at[pl.ds(step * tt, tt)], bufs[slot], sems.at[slot])

    @pl.when(i == 0)
    def _():
        fetch(0, 0).start()

    for slot in range(2):
        @pl.when(i % 2 == slot)
        def _(slot=slot):
            @pl.when(i + 1 < n)
            def _():
                fetch(i + 1, 1 - slot).start()

            fetch(i, slot).wait()
            body(bufs[slot])


def _peer_u_kernel(off_hbm, x_ref, u_ref, o_ref, off_a, off_b, sems):
    tt, nk = off_a.shape
    diag = _diag_mask(SUBLANES * nk)

    def body(off_ref):
        for t in range(tt):
            r = _gather_rows(u_ref, off_ref, t, nk)
            xt = x_ref[pl.ds(t, 1), :].reshape(SUBLANES, LANES).astype(BF16)
            p = lax.dot_general(xt, r, (((1,), (1,)), ((), ())), preferred_element_type=F32)
            o_ref[pl.ds(t, 1), :] = jnp.sum(jnp.where(diag, p, 0.0), axis=0, keepdims=True)

    _with_offsets(off_hbm, (off_a, off_b), sems, body)


def _peer_u(eidx, x2, u_pk):
    m, nk = eidx.shape
    ln = u_pk.shape[1]
    tt = min(EXPERT_TOKENS, m)
    n = SUBLANES * nk
    return pl.pallas_call(
        _peer_u_kernel,
        grid=(m // tt,),
        in_specs=[pl.BlockSpec(memory_space=pl.ANY),
                  pl.BlockSpec((tt, SUBLANES * ln), lambda i: (i, 0)),
                  pl.BlockSpec(u_pk.shape, lambda i: (0, 0), pipeline_mode=pl.Buffered(1))],
        out_specs=pl.BlockSpec((tt, n), lambda i: (i, 0)),
        out_shape=jax.ShapeDtypeStruct((m, n), F32),
        scratch_shapes=[pltpu.SMEM((tt, nk), jnp.int32), pltpu.SMEM((tt, nk), jnp.int32),
                        pltpu.SemaphoreType.DMA((2,))],
        compiler_params=_cp(("arbitrary",)),
        name="peer_u",
    )(eidx, x2, u_pk)


def _dot_exact01(x, w):
    acc = None
    rest = x
    for _ in range(3):
        piece = rest.astype(BF16)
        rest = rest - piece.astype(F32)
        d = jnp.dot(piece, w, preferred_element_type=F32)
        acc = d if acc is None else acc + d
    return acc


def _coef_kernel(part_ref, gate_ref, o_ref):
    n = part_ref.shape[1]
    nk = gate_ref.shape[1]
    sub = n // nk
    fold = (lax.broadcasted_iota(jnp.int32, (n, nk), 0) // sub
            == lax.broadcasted_iota(jnp.int32, (n, nk), 1)).astype(BF16)
    spread = (lax.broadcasted_iota(jnp.int32, (nk, n), 1) // sub
              == lax.broadcasted_iota(jnp.int32, (nk, n), 0)).astype(BF16)
    a = _dot_exact01(part_ref[...], fold)
    gelu = 0.5 * a * (1.0 + lax.erf(a * (1.0 / math.sqrt(2.0))))
    o_ref[...] = _dot_exact01(gate_ref[...] * gelu, spread)


def _coef(part, gate):
    m, n = part.shape
    nk = gate.shape[1]
    tt = min(LN_ROW_TILE, m)
    return pl.pallas_call(
        _coef_kernel,
        grid=(m // tt,),
        in_specs=[pl.BlockSpec((tt, n), lambda i: (i, 0)),
                  pl.BlockSpec((tt, nk), lambda i: (i, 0))],
        out_specs=pl.BlockSpec((tt, n), lambda i: (i, 0)),
        out_shape=jax.ShapeDtypeStruct((m, n), F32),
        compiler_params=_cp(("parallel",)),
        name="peer_coef",
    )(part, gate)


def _peer_v_kernel(off_hbm, c_ref, v_ref, o_ref, off_a, off_b, sems):
    tt, nk = off_a.shape
    diag = _diag_mask(SUBLANES * nk)

    def body(off_ref):
        for t in range(tt):
            r = _gather_rows(v_ref, off_ref, t, nk)
            c8 = jnp.where(diag, c_ref[pl.ds(t, 1), :], 0.0).astype(BF16)
            y = jnp.dot(c8, r, preferred_element_type=F32)
            o_ref[pl.ds(t, 1), :] = y.reshape(1, SUBLANES * LANES)

    _with_offsets(off_hbm, (off_a, off_b), sems, body)


def _peer_v(eidx, coef8, v_pk):
    m, nk = eidx.shape
    ln = v_pk.shape[1]
    tt = min(EXPERT_TOKENS, m)
    n = SUBLANES * nk
    return pl.pallas_call(
        _peer_v_kernel,
        grid=(m // tt,),
        in_specs=[pl.BlockSpec(memory_space=pl.ANY),
                  pl.BlockSpec((tt, n), lambda i: (i, 0)),
                  pl.BlockSpec(v_pk.shape, lambda i: (0, 0), pipeline_mode=pl.Buffered(1))],
        out_specs=pl.BlockSpec((tt, SUBLANES * ln), lambda i: (i, 0)),
        out_shape=jax.ShapeDtypeStruct((m, SUBLANES * ln), F32),
        scratch_shapes=[pltpu.SMEM((tt, nk), jnp.int32), pltpu.SMEM((tt, nk), jnp.int32),
                        pltpu.SemaphoreType.DMA((2,))],
        compiler_params=_cp(("arbitrary",)),
        name="peer_v",
    )(eidx, coef8, v_pk)


def _res_ln_kernel(x_ref, y_ref, g_ref, b_ref, o_ref, *, alpha):
    o_ref[...] = _layer_norm(alpha * x_ref[...] + y_ref[...], g_ref[...], b_ref[...])


def _res_ln(x2, y2, g, b, alpha):
    m, d = x2.shape
    tm = min(LN_ROW_TILE, m)
    row = pl.BlockSpec((tm, d), lambda i: (i, 0))
    const = pl.BlockSpec((1, d), lambda i: (0, 0))
    return pl.pallas_call(
        functools.partial(_res_ln_kernel, alpha=alpha),
        grid=(m // tm,),
        in_specs=[row, row, const, const],
        out_specs=row,
        out_shape=jax.ShapeDtypeStruct((m, d), F32),
        compiler_params=_cp(("parallel",)),
        name="residual_ln",
    )(x2, y2, g, b)


def _peer(x2, wq_bf, keys_bf, u_pk, v_pk, heads, nkeys):
    m, d = x2.shape
    assert d == SUBLANES * LANES, "the expert stages keep one token row per (8, 128) vreg"
    eidx, gate = _route(x2, wq_bf, keys_bf, heads, nkeys)
    part = _peer_u(eidx, x2, u_pk)
    coef8 = _coef(part, gate)
    return _peer_v(eidx, coef8, v_pk)


def _encoder_layer(x, lw, alpha):
    b, l, d = x.shape
    cw = lw["a_conv_w"].shape[1]
    x2 = x.reshape(b * l, d)
    p = _in_proj(x2, lw["w_in"], lw["b_in"])
    ya, x0, z = _mix(p.reshape(b, l, -1), lw["a_conv_w"], lw["h_conv_w"], lw["h_conv_b"], cw)
    yh = _hyena_mix(z, x0, lw["hy_bias"], lw["hf"])
    x1 = _out_proj(ya.reshape(b * l, cw), yh.reshape(b * l, -1), x2, lw["w_out_a"], lw["w_out_h"],
                   lw["ln1_g"], lw["ln1_b"], alpha)
    y = _peer(x1, lw["peer_wq"], lw["peer_keys"], lw["peer_u"], lw["peer_v"], lw["heads"], lw["nkeys"])
    return _res_ln(x1, y, lw["ln2_g"], lw["ln2_b"], alpha).reshape(b, l, d)


def kernel(x_prompt, x_sample, w_in, b_in, a_conv_w, h_conv_w, h_conv_b, hf_w1, hf_b1, hf_freq, hf_w2,
           hf_b2, hf_w3, hy_decay, hy_bias, w_out, ln1_g, ln1_b, peer_wq, peer_keys, peer_u, peer_v,
           ln2_g, ln2_b):
    depth = w_in.shape[0]
    alpha = (2.0 * depth) ** 0.25
    layers = []
    for i in range(depth):
        cw = a_conv_w.shape[2]
        heads, _, nkeys, half = peer_keys.shape[1:]
        layers.append(dict(
            w_in=w_in[i].astype(BF16), b_in=b_in[i][None].astype(F32),
            a_conv_w=a_conv_w[i], h_conv_w=h_conv_w[i], h_conv_b=h_conv_b[i][None],
            hf=(hf_w1[i], hf_b1[i], hf_freq[i], hf_w2[i], hf_b2[i], hf_w3[i], hy_decay[i]),
            hy_bias=hy_bias[i],
            w_out_a=w_out[i, :cw].astype(BF16), w_out_h=w_out[i, cw:].astype(BF16),
            ln1_g=ln1_g[i][None], ln1_b=ln1_b[i][None],
            peer_wq=peer_wq[i].astype(BF16),
            peer_keys=peer_keys[i].reshape(heads * 2, nkeys, half).astype(BF16),
            peer_u=_pack_table(peer_u[i]), peer_v=_pack_table(peer_v[i]),
            ln2_g=ln2_g[i][None], ln2_b=ln2_b[i][None], heads=heads, nkeys=nkeys))

    def trunk(x):
        for lw in layers:
            x = _encoder_layer(x, lw, alpha)
        return x

    return trunk(x_prompt), trunk(x_sample)
```

```python
import functools
import math

import jax
import jax.numpy as jnp
from jax import lax
from jax.experimental import pallas as pl
from jax.experimental.pallas import tpu as pltpu

F32 = jnp.float32
BF16 = jnp.bfloat16
HI = lax.Precision.HIGHEST

LANES = 128
SUBLANES = 8
PACKED_ROWS = SUBLANES // 2
SHORT_K = 3
PEER_TOPK = 16
LN_EPS = 1e-5
DFT_N2 = 128

ROW_TILE = 1024
LN_ROW_TILE = 1024
DFT_COL_TILE = 4096
SPECTRAL_BATCH = 4
ROUTE_TOKENS = 512
PACK_EXPERTS = 512
EXPERT_TOKENS = 64
VMEM_LIMIT = 56 * 1024 * 1024


def _cp(sem, vmem=VMEM_LIMIT):
    return pltpu.CompilerParams(dimension_semantics=sem, vmem_limit_bytes=vmem)


def _in_proj_kernel(x_ref, w_ref, b_ref, o_ref):
    o_ref[...] = jnp.dot(x_ref[...].astype(BF16), w_ref[...],
                         preferred_element_type=F32) + b_ref[...]


def _in_proj(x2, w_bf, b):
    m, d = x2.shape
    n = w_bf.shape[1]
    tm = min(ROW_TILE, m)
    return pl.pallas_call(
        _in_proj_kernel,
        grid=(m // tm,),
        in_specs=[pl.BlockSpec((tm, d), lambda i: (i, 0)),
                  pl.BlockSpec((d, n), lambda i: (0, 0)),
                  pl.BlockSpec((1, n), lambda i: (0, 0))],
        out_specs=pl.BlockSpec((tm, n), lambda i: (i, 0)),
        out_shape=jax.ShapeDtypeStruct((m, n), F32),
        compiler_params=_cp(("parallel",)),
        name="in_proj",
    )(x2, w_bf, b)


def _mix_kernel(pm_ref, pp_ref, pn_ref, aw_ref, hw_ref, hb_ref, ya_ref, x0_ref, z_ref, *, cw):
    i = pl.program_id(1)
    n = pl.num_programs(1)
    tl = pm_ref.shape[1]
    rows = lax.broadcasted_iota(jnp.int32, (tl, 1), 0)
    has_prev = i > 0
    has_next = i < n - 1

    def conv3(cur, prev_row, next_row, w):
        prev_row = jnp.where(has_prev, prev_row, 0.0)
        next_row = jnp.where(has_next, next_row, 0.0)
        dn = jnp.where(rows == 0, prev_row, pltpu.roll(cur, 1, 0))
        up = jnp.where(rows == tl - 1, next_row, pltpu.roll(cur, tl - 1, 0))
        return dn * w[0:1] + cur * w[1:2] + up * w[2:3]

    def sec(ref, r0, r1, j):
        return ref[0, r0:r1, j * cw:(j + 1) * cw]

    g = sec(pm_ref, 0, tl, 1) * sec(pm_ref, 0, tl, 2)
    gp = sec(pp_ref, 7, 8, 1) * sec(pp_ref, 7, 8, 2)
    gn = sec(pn_ref, 0, 1, 1) * sec(pn_ref, 0, 1, 2)
    ya_ref[0] = sec(pm_ref, 0, tl, 0) * conv3(g, gp, gn, aw_ref[...])

    def hconv(j):
        c = 3 + j
        w = hw_ref[:, j * cw:(j + 1) * cw]
        return conv3(sec(pm_ref, 0, tl, c), sec(pp_ref, 7, 8, c), sec(pn_ref, 0, 1, c), w) \
            + hb_ref[:, j * cw:(j + 1) * cw]

    x0_ref[0] = hconv(0)
    z_ref[0] = hconv(2) * hconv(1)


def _mix(p3, a_conv_w, h_conv_w, h_conv_b, cw):
    b, l, pw = p3.shape
    tl = min(ROW_TILE, l)
    nl = l // tl
    r8 = tl // 8
    out = jax.ShapeDtypeStruct((b, l, cw), F32)
    ospec = pl.BlockSpec((1, tl, cw), lambda bi, i: (bi, i, 0))
    return pl.pallas_call(
        functools.partial(_mix_kernel, cw=cw),
        grid=(b, nl),
        in_specs=[pl.BlockSpec((1, tl, pw), lambda bi, i: (bi, i, 0)),
                  pl.BlockSpec((1, 8, pw), lambda bi, i: (bi, jnp.maximum(i * r8 - 1, 0), 0)),
                  pl.BlockSpec((1, 8, pw), lambda bi, i: (bi, jnp.minimum((i + 1) * r8, l // 8 - 1), 0)),
                  pl.BlockSpec((SHORT_K, cw), lambda bi, i: (0, 0)),
                  pl.BlockSpec((SHORT_K, 3 * cw), lambda bi, i: (0, 0)),
                  pl.BlockSpec((1, 3 * cw), lambda bi, i: (0, 0))],
        out_specs=[ospec, ospec, ospec],
        out_shape=[out, out, out],
        compiler_params=_cp(("parallel", "parallel")),
        name="mix",
    )(p3, p3, p3, a_conv_w, h_conv_w, h_conv_b)


def _filter_kernel(t_ref, w_ref, f_ref, w1t_ref, w1c_ref, w1s_ref, b1_ref, fr_ref, w2_ref, b2_ref,
                   w3_ref, dec_ref, hk_ref, asum_ref, *, hw):
    i = pl.program_id(0)
    tl = t_ref.shape[0]
    t = t_ref[...]
    ang = w_ref[...] * f_ref[...]
    z1 = (t * w1t_ref[...]
          + jnp.dot(jnp.cos(ang), w1c_ref[...], precision=HI, preferred_element_type=F32)
          + jnp.dot(-jnp.sin(ang), w1s_ref[...], precision=HI, preferred_element_type=F32)
          + b1_ref[...])
    fr = fr_ref[...]
    h = jnp.sin(fr * z1)
    h = jnp.sin(fr * (jnp.dot(h, w2_ref[...], precision=HI, preferred_element_type=F32) + b2_ref[...]))
    h = jnp.dot(h, w3_ref[...], precision=HI, preferred_element_type=F32)
    h = h * jnp.exp(-t * jnp.abs(dec_ref[...]))
    rows = i * tl + lax.broadcasted_iota(jnp.int32, (tl, 1), 0)
    fwd = h[:, :hw]
    bwd = jnp.where(rows == 0, 0.0, h[:, hw:])
    hk_ref[0] = fwd
    hk_ref[1] = bwd
    s = jnp.concatenate([jnp.sum(jnp.abs(fwd), axis=0, keepdims=True),
                         jnp.sum(jnp.abs(bwd), axis=0, keepdims=True)], axis=0)

    @pl.when(i == 0)
    def _():
        asum_ref[...] = jnp.zeros_like(asum_ref)

    asum_ref[...] += s


def _pad2(a, r, c):
    return jnp.zeros((r, c), F32).at[:a.shape[0], :a.shape[1]].set(a.astype(F32))


def _hyena_filter(l, hf_w1, hf_b1, hf_freq, hf_w2, hf_b2, hf_w3, hy_decay):
    bands = (hf_w1.shape[0] - 1) // 2
    assert bands <= LANES and hf_w1.shape[1] <= LANES, "filter features and hidden width are padded to one lane tile"
    hw = hy_decay.shape[1]
    t = jnp.linspace(0.0, 1.0, l, dtype=F32)[:, None]
    w = (2.0 * math.pi) * jnp.arange(l, dtype=F32)[:, None] / l
    f = jnp.linspace(1e-4, bands - 1, bands, dtype=F32)[None, :]
    p = LANES
    tl = min(ROW_TILE, l)
    args = (t, w, _pad2(f, 1, p), _pad2(hf_w1[0:1], 1, p), _pad2(hf_w1[1:1 + bands], p, p),
            _pad2(hf_w1[1 + bands:], p, p), _pad2(hf_b1[None], 1, p), _pad2(hf_freq[None], 1, p),
            _pad2(hf_w2, p, p), _pad2(hf_b2[None], 1, p), _pad2(hf_w3, p, 2 * hw),
            hy_decay.reshape(1, 2 * hw).astype(F32))
    const = lambda shape: pl.BlockSpec(shape, lambda i: (0, 0))
    return pl.pallas_call(
        functools.partial(_filter_kernel, hw=hw),
        grid=(l // tl,),
        in_specs=[pl.BlockSpec((tl, 1), lambda i: (i, 0)), pl.BlockSpec((tl, 1), lambda i: (i, 0)),
                  const((1, p)), const((1, p)), const((p, p)), const((p, p)), const((1, p)),
                  const((1, p)), const((p, p)), const((1, p)), const((p, 2 * hw)), const((1, 2 * hw))],
        out_specs=[pl.BlockSpec((2, tl, hw), lambda i: (0, i, 0)),
                   pl.BlockSpec((2, hw), lambda i: (0, 0))],
        out_shape=[jax.ShapeDtypeStruct((2, l, hw), F32), jax.ShapeDtypeStruct((2, hw), F32)],
        compiler_params=_cp(("arbitrary",)),
        name="hyena_filter",
    )(*args)


def _dft_consts(l):
    n2s = DFT_N2
    n = 2 * l
    n1s = n // n2s
    two_pi = 2.0 * math.pi
    k1 = jnp.arange(n1s, dtype=jnp.int32)[:, None]
    n1 = jnp.arange(n1s // 2, dtype=jnp.int32)[None, :]
    ang = -two_pi * ((k1 * n1) % n1s).astype(F32) / n1s
    fa = jnp.concatenate([jnp.cos(ang), jnp.sin(ang)], axis=0)
    k1 = jnp.arange(n1s, dtype=jnp.int32)[:, None, None]
    k2 = jnp.arange(n2s, dtype=jnp.int32)[None, :, None]
    n2 = jnp.arange(n2s, dtype=jnp.int32)[None, None, :]
    ang = -two_pi * ((n2 * k2 * n1s + n2 * k1) % n).astype(F32) / n
    mr, mi = jnp.cos(ang), jnp.sin(ang)
    g = jnp.concatenate([jnp.concatenate([mr, -mi], 2), jnp.concatenate([mi, mr], 2)], 1)
    mrt, mit = jnp.swapaxes(mr, 1, 2), jnp.swapaxes(mi, 1, 2)
    gi = jnp.concatenate([jnp.concatenate([mrt, mit], 2), jnp.concatenate([-mit, mrt], 2)], 1)
    n1 = jnp.arange(n1s // 2, dtype=jnp.int32)[:, None]
    k1 = jnp.arange(n1s, dtype=jnp.int32)[None, :]
    ang = two_pi * ((n1 * k1) % n1s).astype(F32) / n1s
    fb = jnp.concatenate([jnp.cos(ang), -jnp.sin(ang)], axis=1) / n
    return fa, g, gi, fb


def _mm(a, b, precise):
    dot = lambda x, y: jnp.dot(x, y, preferred_element_type=F32)
    a_hi, b_hi = a.astype(BF16), b.astype(BF16)
    if not precise:
        return dot(a_hi, b_hi)
    a_lo = (a - a_hi.astype(F32)).astype(BF16)
    b_lo = (b - b_hi.astype(F32)).astype(BF16)
    return dot(a_hi, b_hi) + (dot(a_hi, b_lo) + dot(a_lo, b_hi))


def _lmat_kernel(f_ref, z_ref, o_ref, *, precise):
    o_ref[0] = _mm(f_ref[...], z_ref[0], precise).astype(o_ref.dtype)


def _dft_stage1(fa, z3, precise):
    b, r, cols = z3.shape
    m = fa.shape[0]
    tn = min(DFT_COL_TILE, cols)
    return pl.pallas_call(
        functools.partial(_lmat_kernel, precise=precise),
        grid=(b, cols // tn),
        in_specs=[pl.BlockSpec((m, r), lambda bi, j: (0, 0)),
                  pl.BlockSpec((1, r, tn), lambda bi, j: (bi, 0, j))],
        out_specs=pl.BlockSpec((1, m, tn), lambda bi, j: (bi, 0, j)),
        out_shape=jax.ShapeDtypeStruct((b, m, cols), F32 if precise else BF16),
        compiler_params=_cp(("parallel", "parallel")),
        name="dft_stage1",
    )(fa, z3)


def _kf_kernel(a_ref, g_ref, asum_ref, kf_ref):
    n2 = a_ref.shape[3]
    c = a_ref.shape[4]
    g = g_ref[0]
    f0 = _mm(g, a_ref[0, :, 0].reshape(2 * n2, c), True)
    f1 = _mm(g, a_ref[1, :, 0].reshape(2 * n2, c), True)
    inv = 1.0 / (asum_ref[0:1] + asum_ref[1:2])
    kf_ref[0] = jnp.concatenate([(f0[:n2] + f1[:n2]) * inv, (f0[n2:] - f1[n2:]) * inv], axis=0)


def _filter_spectrum(a5, g, asum):
    _, _, n1s, n2, c = a5.shape
    return pl.pallas_call(
        _kf_kernel,
        grid=(n1s,),
        in_specs=[pl.BlockSpec((2, 2, 1, n2, c), lambda k: (0, 0, k, 0, 0)),
                  pl.BlockSpec((1, 2 * n2, 2 * n2), lambda k: (k, 0, 0)),
                  pl.BlockSpec((2, c), lambda k: (0, 0))],
        out_specs=pl.BlockSpec((1, 2 * n2, c), lambda k: (k, 0, 0)),
        out_shape=jax.ShapeDtypeStruct((n1s, 2 * n2, c), F32),
        compiler_params=_cp(("parallel",)),
        name="filter_spectrum",
    )(a5, g, asum)


def _spec_kernel(a_ref, g_ref, kf_ref, gi_ref, d_ref):
    n2 = a_ref.shape[3]
    c = a_ref.shape[4]
    kf = kf_ref[0]
    kr, ki = kf[:n2], kf[n2:]
    for bi in range(a_ref.shape[0]):
        a = a_ref[bi, :, 0].reshape(2 * n2, c)
        xf = _mm(g_ref[0], a, False)
        xr, xi = xf[:n2], xf[n2:]
        y = jnp.concatenate([xr * kr - xi * ki, xr * ki + xi * kr], axis=0)
        d = _mm(gi_ref[0], y, False)
        d_ref[bi, :, 0] = d.reshape(2, n2, c).astype(d_ref.dtype)


def _spectral_multiply(a5, g, kf, gi):
    b, _, n1s, n2, c = a5.shape
    bb = math.gcd(b, SPECTRAL_BATCH)
    blk = pl.BlockSpec((bb, 2, 1, n2, c), lambda k, bi: (bi, 0, k, 0, 0))
    mat = pl.BlockSpec((1, 2 * n2, 2 * n2), lambda k, bi: (k, 0, 0))
    return pl.pallas_call(
        _spec_kernel,
        grid=(n1s, b // bb),
        in_specs=[blk, mat, pl.BlockSpec((1, 2 * n2, c), lambda k, bi: (k, 0, 0)), mat],
        out_specs=blk,
        out_shape=jax.ShapeDtypeStruct(a5.shape, a5.dtype),
        compiler_params=_cp(("parallel", "arbitrary")),
        name="spectral_multiply",
    )(a5, g, kf, gi)


def _inv_kernel(f_ref, d_ref, z_ref, x0_ref, bias_ref, o_ref):
    y = _mm(f_ref[...], d_ref[0], False)
    o_ref[0] = x0_ref[0] * (y + z_ref[0] * bias_ref[...])


def _dft_final(fb, d3, z3, x03, bias_t):
    b, m2, cols = d3.shape
    r = fb.shape[0]
    tn = min(DFT_COL_TILE, cols)
    rspec = pl.BlockSpec((1, r, tn), lambda bi, j: (bi, 0, j))
    return pl.pallas_call(
        _inv_kernel,
        grid=(b, cols // tn),
        in_specs=[pl.BlockSpec((r, m2), lambda bi, j: (0, 0)),
                  pl.BlockSpec((1, m2, tn), lambda bi, j: (bi, 0, j)),
                  rspec, rspec,
                  pl.BlockSpec((1, tn), lambda bi, j: (0, j))],
        out_specs=rspec,
        out_shape=jax.ShapeDtypeStruct((b, r, cols), F32),
        compiler_params=_cp(("parallel", "parallel")),
        name="dft_final",
    )(fb, d3, z3, x03, bias_t)


def _hyena_mix(z, x0, hy_bias, hf):
    b, l, c = z.shape
    n2 = DFT_N2
    n1s = 2 * l // n2
    fa, g, gi, fb = _dft_consts(l)
    hk, asum = _hyena_filter(l, *hf)
    cols = n2 * c
    ka = _dft_stage1(fa, hk.reshape(2, n1s // 2, cols), True)
    kf = _filter_spectrum(ka.reshape(2, 2, n1s, n2, c), g, asum)
    z3 = z.reshape(b, n1s // 2, cols)
    a = _dft_stage1(fa.astype(BF16), z3, False)
    d = _spectral_multiply(a.reshape(b, 2, n1s, n2, c), g.astype(BF16), kf, gi.astype(BF16))
    bias_t = jnp.tile(hy_bias.astype(F32), n2)[None, :]
    yh = _dft_final(fb.astype(BF16), d.reshape(b, 2 * n1s, cols), z3, x0.reshape(b, n1s // 2, cols), bias_t)
    return yh.reshape(b, l, c)


def _layer_norm(r, g, b):
    mu = jnp.mean(r, axis=-1, keepdims=True)
    rc = r - mu
    var = jnp.mean(rc * rc, axis=-1, keepdims=True)
    return rc * lax.rsqrt(var + LN_EPS) * g + b


def _out_proj_kernel(ya_ref, yh_ref, x_ref, wa_ref, wh_ref, g_ref, b_ref, o_ref, *, alpha):
    mix = (jnp.dot(ya_ref[...].astype(BF16), wa_ref[...], preferred_element_type=F32)
           + jnp.dot(yh_ref[...].astype(BF16), wh_ref[...], preferred_element_type=F32))
    o_ref[...] = _layer_norm(alpha * x_ref[...] + mix, g_ref[...], b_ref[...])


def _out_proj(ya, yh, x2, wa_bf, wh_bf, g, b, alpha):
    m, d = x2.shape
    cw = ya.shape[1]
    tm = min(ROW_TILE, m)
    row = lambda w: pl.BlockSpec((tm, w), lambda i: (i, 0))
    const = lambda shape: pl.BlockSpec(shape, lambda i: (0, 0))
    return pl.pallas_call(
        functools.partial(_out_proj_kernel, alpha=alpha),
        grid=(m // tm,),
        in_specs=[row(cw), row(cw), row(d), const((cw, d)), const((cw, d)), const((1, d)), const((1, d))],
        out_specs=row(d),
        out_shape=jax.ShapeDtypeStruct((m, d), F32),
        compiler_params=_cp(("parallel",)),
        name="out_proj_ln",
    )(ya, yh, x2, wa_bf, wh_bf, g, b)


NEG_INF = float("-inf")
NO_ROW = float(2 ** 30)


def _topk_keys(s, k):
    n, t = s.shape
    sub = lax.broadcasted_iota(jnp.int32, (SUBLANES, t), 0).astype(F32)
    slabs = [s[g * SUBLANES:(g + 1) * SUBLANES] for g in range(n // SUBLANES)]
    ids = [sub + float(g * SUBLANES) for g in range(n // SUBLANES)]
    vals, poss = [], []
    for _ in range(k):
        vs, rs = slabs, ids
        while len(vs) > 1:
            nv, nr = [], []
            for a in range(0, len(vs), 2):
                keep = vs[a] >= vs[a + 1]
                nv.append(jnp.maximum(vs[a], vs[a + 1]))
                nr.append(jnp.where(keep, rs[a], rs[a + 1]))
            vs, rs = nv, nr
        m = jnp.max(vs[0], axis=0, keepdims=True)
        pos = jnp.min(jnp.where(vs[0] == m, rs[0], NO_ROW), axis=0, keepdims=True)
        vals.append(m)
        poss.append(pos)
        slabs = [jnp.where(r == pos, NEG_INF, v) for v, r in zip(slabs, ids)]
    return jnp.concatenate(vals, axis=0), jnp.concatenate(poss, axis=0)


def _topk_ordered(s, order, payload, k):
    vals, pays = [], []
    for _ in range(k):
        m = jnp.max(s, axis=0, keepdims=True)
        first = jnp.min(jnp.where(s == m, order, NO_ROW), axis=0, keepdims=True)
        hit = order == first
        vals.append(m)
        pays.append(jnp.max(jnp.where(hit, payload, -1.0), axis=0, keepdims=True))
        s = jnp.where(hit, NEG_INF, s)
    return jnp.concatenate(vals, axis=0), jnp.concatenate(pays, axis=0)


def _pair_candidates(v1, i1, v2, i2, nkeys):
    kk, t = v1.shape
    up8 = lambda x: -(-x // SUBLANES) * SUBLANES
    lim = lambda a: kk // (a + 1)
    nfull = sum(1 for a in range(kk) if lim(a) >= SUBLANES)
    blocks = [("a", a, 0, lim(a)) for a in range(nfull)]
    blocks += [("b", b, nfull, lim(b)) for b in range(nfull)]
    blocks += [("a", a, nfull, lim(a)) for a in range(nfull, kk) if lim(a) > nfull]
    assert sum(hi - lo for _, _, lo, hi in blocks) == sum(lim(a) for a in range(kk))
    sums, poss, eids = [], [], []
    for kind, fixed, lo, hi in blocks:
        rows = up8(hi)
        it = lax.broadcasted_iota(jnp.int32, (rows, t), 0).astype(F32)
        valid = (it >= float(lo)) & (it < float(hi))
        if kind == "a":
            sm = v1[fixed:fixed + 1] + v2[:rows]
            ps = it + float(fixed * kk)
            ei = i1[fixed:fixed + 1] * float(nkeys) + i2[:rows]
        else:
            sm = v1[:rows] + v2[fixed:fixed + 1]
            ps = it * float(kk) + float(fixed)
            ei = i1[:rows] * float(nkeys) + i2[fixed:fixed + 1]
        sums.append(jnp.where(valid, sm, NEG_INF))
        poss.append(ps)
        eids.append(ei)
    cat = lambda xs: jnp.concatenate(xs, axis=0)
    return cat(sums), cat(poss), cat(eids)


def _route_kernel(x_ref, wq_ref, keys_ref, eidx_ref, gate_ref, q_scr, e_scr, g_scr, *, heads, nkeys, row_scale):
    kk = PEER_TOPK
    half = keys_ref.shape[2]
    q_scr[...] = jnp.dot(x_ref[...].astype(BF16), wq_ref[...], preferred_element_type=F32)

    def sub_key_tops(h):
        tops = []
        for p in range(2):
            c = h * 2 + p
            qc = q_scr[:, pl.ds(pl.multiple_of(c * half, half), half)].astype(BF16)
            st = lax.dot_general(keys_ref[c], qc, (((1,), (1,)), ((), ())),
                                 preferred_element_type=F32)
            tops.extend(_topk_keys(st, kk))
        return tuple(tops)

    def pick_experts(h, tops):
        v1, i1, v2, i2 = tops
        cand, cpos, cidx = _pair_candidates(v1, i1, v2, i2, nkeys)
        best, e = _topk_ordered(cand, cpos, cidx, kk)
        ex = jnp.exp(best - jnp.max(best, axis=0, keepdims=True))
        gate = ex / jnp.sum(ex, axis=0, keepdims=True)
        r0 = pl.multiple_of(h * kk, kk)
        e_scr[pl.ds(r0, kk), :] = e
        g_scr[pl.ds(r0, kk), :] = gate

    def step(h, tops):
        nxt = sub_key_tops(h + 1)
        pick_experts(h, tops)
        return nxt

    last = lax.fori_loop(0, heads - 1, step, sub_key_tops(0))
    pick_experts(heads - 1, last)
    eidx_ref[...] = (e_scr[...].T * float(row_scale)).astype(jnp.int32)
    gate_ref[...] = g_scr[...].T


def _route(x2, wq_bf, keys_bf, heads, nkeys):
    m, d = x2.shape
    qd = wq_bf.shape[1]
    tt = min(ROUTE_TOKENS, m)
    hk = heads * PEER_TOPK
    return pl.pallas_call(
        functools.partial(_route_kernel, heads=heads, nkeys=nkeys, row_scale=PACKED_ROWS),
        grid=(m // tt,),
        in_specs=[pl.BlockSpec((tt, d), lambda i: (i, 0)),
                  pl.BlockSpec((d, qd), lambda i: (0, 0)),
                  pl.BlockSpec(keys_bf.shape, lambda i: (0, 0, 0))],
        out_specs=[pl.BlockSpec((tt, hk), lambda i: (i, 0)),
                   pl.BlockSpec((tt, hk), lambda i: (i, 0))],
        out_shape=[jax.ShapeDtypeStruct((m, hk), jnp.int32), jax.ShapeDtypeStruct((m, hk), F32)],
        scratch_shapes=[pltpu.VMEM((tt, qd), F32), pltpu.VMEM((hk, tt), F32), pltpu.VMEM((hk, tt), F32)],
        compiler_params=_cp(("parallel",)),
        name="peer_route",
    )(x2, wq_bf, keys_bf)


def _pack_table(tab):
    e, d = tab.shape
    rows = d // (2 * LANES)
    eb = math.gcd(e, PACK_EXPERTS)
    return pl.pallas_call(
        _pack_kernel,
        grid=(e // eb,),
        in_specs=[pl.BlockSpec((eb, d), lambda i: (i, 0))],
        out_specs=pl.BlockSpec((eb * rows, LANES), lambda i: (i, 0)),
        out_shape=jax.ShapeDtypeStruct((e * rows, LANES), jnp.int32),
        compiler_params=_cp(("parallel",)),
        name="pack_table",
    )(tab.astype(F32))


def _pack_kernel(t_ref, o_ref):
    eb, d = t_ref.shape
    rows = d // (2 * LANES)
    bf16_bits = lambda x: lax.bitcast_convert_type(x.astype(BF16).astype(F32), jnp.int32)
    for j in range(rows):
        lo = bf16_bits(t_ref[:, 2 * j * LANES:(2 * j + 1) * LANES])
        hi = bf16_bits(t_ref[:, (2 * j + 1) * LANES:(2 * j + 2) * LANES])
        word = (hi & jnp.int32(-65536)) | ((lo >> 16) & jnp.int32(0xFFFF))
        o_ref[pl.ds(j, eb, stride=rows), :] = word


def _gather_rows(tab_ref, off_ref, t, nk):
    tiles = [tab_ref[pl.ds(pl.multiple_of(off_ref[t, k], PACKED_ROWS), PACKED_ROWS), :] for k in range(nk)]
    return pltpu.bitcast(jnp.concatenate(tiles, axis=0), BF16)


def _diag_mask(n):
    lane = lax.broadcasted_iota(jnp.int32, (SUBLANES, n), 1)
    return (lane & (SUBLANES - 1)) == lax.broadcasted_iota(jnp.int32, (SUBLANES, n), 0)


def _with_offsets(off_hbm, bufs, sems, body):
    i = pl.program_id(0)
    n = pl.num_programs(0)
    tt = bufs[0].shape[0]

    def fetch(step, slot):
        return pltpu.make_async_copy(off_hbm.at[pl.ds(step * tt, tt)], bufs[slot], sems.at[slot])

    @pl.when(i == 0)
    def _():
        fetch(0, 0).start()

    for slot in range(2):
        @pl.when(i % 2 == slot)
        def _(slot=slot):
            @pl.when(i + 1 < n)
            def _():
                fetch(i + 1, 1 - slot).start()

            fetch(i, slot).wait()
            body(bufs[slot])


def _peer_u_kernel(off_hbm, x_ref, u_ref, o_ref, off_a, off_b, sems):
    tt, nk = off_a.shape
    diag = _diag_mask(SUBLANES * nk)

    def body(off_ref):
        for t in range(tt):
            r = _gather_rows(u_ref, off_ref, t, nk)
            xt = x_ref[pl.ds(t, 1), :].reshape(SUBLANES, LANES).astype(BF16)
            p = lax.dot_general(xt, r, (((1,), (1,)), ((), ())), preferred_element_type=F32)
            o_ref[pl.ds(t, 1), :] = jnp.sum(jnp.where(diag, p, 0.0), axis=0, keepdims=True)

    _with_offsets(off_hbm, (off_a, off_b), sems, body)


def _peer_u(eidx, x2, u_pk):
    m, nk = eidx.shape
    ln = u_pk.shape[1]
    tt = min(EXPERT_TOKENS, m)
    n = SUBLANES * nk
    return pl.pallas_call(
        _peer_u_kernel,
        grid=(m // tt,),
        in_specs=[pl.BlockSpec(memory_space=pl.ANY),
                  pl.BlockSpec((tt, SUBLANES * ln), lambda i: (i, 0)),
                  pl.BlockSpec(u_pk.shape, lambda i: (0, 0), pipeline_mode=pl.Buffered(1))],
        out_specs=pl.BlockSpec((tt, n), lambda i: (i, 0)),
        out_shape=jax.ShapeDtypeStruct((m, n), F32),
        scratch_shapes=[pltpu.SMEM((tt, nk), jnp.int32), pltpu.SMEM((tt, nk), jnp.int32),
                        pltpu.SemaphoreType.DMA((2,))],
        compiler_params=_cp(("arbitrary",)),
        name="peer_u",
    )(eidx, x2, u_pk)


def _dot_exact01(x, w):
    acc = None
    rest = x
    for _ in range(3):
        piece = rest.astype(BF16)
        rest = rest - piece.astype(F32)
        d = jnp.dot(piece, w, preferred_element_type=F32)
        acc = d if acc is None else acc + d
    return acc


def _coef_kernel(part_ref, gate_ref, o_ref):
    n = part_ref.shape[1]
    nk = gate_ref.shape[1]
    sub = n // nk
    fold = (lax.broadcasted_iota(jnp.int32, (n, nk), 0) // sub
            == lax.broadcasted_iota(jnp.int32, (n, nk), 1)).astype(BF16)
    spread = (lax.broadcasted_iota(jnp.int32, (nk, n), 1) // sub
              == lax.broadcasted_iota(jnp.int32, (nk, n), 0)).astype(BF16)
    a = _dot_exact01(part_ref[...], fold)
    gelu = 0.5 * a * (1.0 + lax.erf(a * (1.0 / math.sqrt(2.0))))
    o_ref[...] = _dot_exact01(gate_ref[...] * gelu, spread)


def _coef(part, gate):
    m, n = part.shape
    nk = gate.shape[1]
    tt = min(LN_ROW_TILE, m)
    return pl.pallas_call(
        _coef_kernel,
        grid=(m // tt,),
        in_specs=[pl.BlockSpec((tt, n), lambda i: (i, 0)),
                  pl.BlockSpec((tt, nk), lambda i: (i, 0))],
        out_specs=pl.BlockSpec((tt, n), lambda i: (i, 0)),
        out_shape=jax.ShapeDtypeStruct((m, n), F32),
        compiler_params=_cp(("parallel",)),
        name="peer_coef",
    )(part, gate)


def _peer_v_kernel(off_hbm, c_ref, v_ref, o_ref, off_a, off_b, sems):
    tt, nk = off_a.shape
    diag = _diag_mask(SUBLANES * nk)

    def body(off_ref):
        for t in range(tt):
            r = _gather_rows(v_ref, off_ref, t, nk)
            c8 = jnp.where(diag, c_ref[pl.ds(t, 1), :], 0.0).astype(BF16)
            y = jnp.dot(c8, r, preferred_element_type=F32)
            o_ref[pl.ds(t, 1), :] = y.reshape(1, SUBLANES * LANES)

    _with_offsets(off_hbm, (off_a, off_b), sems, body)


def _peer_v(eidx, coef8, v_pk):
    m, nk = eidx.shape
    ln = v_pk.shape[1]
    tt = min(EXPERT_TOKENS, m)
    n = SUBLANES * nk
    return pl.pallas_call(
        _peer_v_kernel,
        grid=(m // tt,),
        in_specs=[pl.BlockSpec(memory_space=pl.ANY),
                  pl.BlockSpec((tt, n), lambda i: (i, 0)),
                  pl.BlockSpec(v_pk.shape, lambda i: (0, 0), pipeline_mode=pl.Buffered(1))],
        out_specs=pl.BlockSpec((tt, SUBLANES * ln), lambda i: (i, 0)),
        out_shape=jax.ShapeDtypeStruct((m, SUBLANES * ln), F32),
        scratch_shapes=[pltpu.SMEM((tt, nk), jnp.int32), pltpu.SMEM((tt, nk), jnp.int32),
                        pltpu.SemaphoreType.DMA((2,))],
        compiler_params=_cp(("arbitrary",)),
        name="peer_v",
    )(eidx, coef8, v_pk)


def _res_ln_kernel(x_ref, y_ref, g_ref, b_ref, o_ref, *, alpha):
    o_ref[...] = _layer_norm(alpha * x_ref[...] + y_ref[...], g_ref[...], b_ref[...])


def _res_ln(x2, y2, g, b, alpha):
    m, d = x2.shape
    tm = min(LN_ROW_TILE, m)
    row = pl.BlockSpec((tm, d), lambda i: (i, 0))
    const = pl.BlockSpec((1, d), lambda i: (0, 0))
    return pl.pallas_call(
        functools.partial(_res_ln_kernel, alpha=alpha),
        grid=(m // tm,),
        in_specs=[row, row, const, const],
        out_specs=row,
        out_shape=jax.ShapeDtypeStruct((m, d), F32),
        compiler_params=_cp(("parallel",)),
        name="residual_ln",
    )(x2, y2, g, b)


def _peer(x2, wq_bf, keys_bf, u_pk, v_pk, heads, nkeys):
    m, d = x2.shape
    assert d == SUBLANES * LANES, "the expert stages keep one token row per (8, 128) vreg"
    eidx, gate = _route(x2, wq_bf, keys_bf, heads, nkeys)
    part = _peer_u(eidx, x2, u_pk)
    coef8 = _coef(part, gate)
    return _peer_v(eidx, coef8, v_pk)


def _encoder_layer(x, lw, alpha):
    b, l, d = x.shape
    cw = lw["a_conv_w"].shape[1]
    x2 = x.reshape(b * l, d)
    p = _in_proj(x2, lw["w_in"], lw["b_in"])
    ya, x0, z = _mix(p.reshape(b, l, -1), lw["a_conv_w"], lw["h_conv_w"], lw["h_conv_b"], cw)
    yh = _hyena_mix(z, x0, lw["hy_bias"], lw["hf"])
    x1 = _out_proj(ya.reshape(b * l, cw), yh.reshape(b * l, -1), x2, lw["w_out_a"], lw["w_out_h"],
                   lw["ln1_g"], lw["ln1_b"], alpha)
    y = _peer(x1, lw["peer_wq"], lw["peer_keys"], lw["peer_u"], lw["peer_v"], lw["heads"], lw["nkeys"])
    return _res_ln(x1, y, lw["ln2_g"], lw["ln2_b"], alpha).reshape(b, l, d)


def kernel(x_prompt, x_sample, w_in, b_in, a_conv_w, h_conv_w, h_conv_b, hf_w1, hf_b1, hf_freq, hf_w2,
           hf_b2, hf_w3, hy_decay, hy_bias, w_out, ln1_g, ln1_b, peer_wq, peer_keys, peer_u, peer_v,
           ln2_g, ln2_b):
    depth = w_in.shape[0]
    alpha = (2.0 * depth) ** 0.25
    layers = []
    for i in range(depth):
        cw = a_conv_w.shape[2]
        heads, _, nkeys, half = peer_keys.shape[1:]
        layers.append(dict(
            w_in=w_in[i].astype(BF16), b_in=b_in[i][None].astype(F32),
            a_conv_w=a_conv_w[i], h_conv_w=h_conv_w[i], h_conv_b=h_conv_b[i][None],
            hf=(hf_w1[i], hf_b1[i], hf_freq[i], hf_w2[i], hf_b2[i], hf_w3[i], hy_decay[i]),
            hy_bias=hy_bias[i],
            w_out_a=w_out[i, :cw].astype(BF16), w_out_h=w_out[i, cw:].astype(BF16),
            ln1_g=ln1_g[i][None], ln1_b=ln1_b[i][None],
            peer_wq=peer_wq[i].astype(BF16),
            peer_keys=peer_keys[i].reshape(heads * 2, nkeys, half).astype(BF16),
            peer_u=_pack_table(peer_u[i]), peer_v=_pack_table(peer_v[i]),
            ln2_g=ln2_g[i][None], ln2_b=ln2_b[i][None], heads=heads, nkeys=nkeys))

    def trunk(x):
        for lw in layers:
            x = _encoder_layer(x, lw, alpha)
        return x

    return trunk(x_prompt), trunk(x_sample)
```

```python
import functools
import math

import jax
import jax.numpy as jnp
from jax import lax
from jax.experimental import pallas as pl
from jax.experimental.pallas import tpu as pltpu

F32 = jnp.float32
BF16 = jnp.bfloat16
HI = lax.Precision.HIGHEST

LANES = 128
SUBLANES = 8
PACKED_ROWS = SUBLANES // 2
SHORT_K = 3
PEER_TOPK = 16
LN_EPS = 1e-5
DFT_N2 = 128

ROW_TILE = 1024
LN_ROW_TILE = 1024
DFT_COL_TILE = 4096
SPECTRAL_BATCH = 4
ROUTE_TOKENS = 512
PACK_EXPERTS = 512
EXPERT_TOKENS = 128
VMEM_LIMIT = 56 * 1024 * 1024


def _cp(sem, vmem=VMEM_LIMIT):
    return pltpu.CompilerParams(dimension_semantics=sem, vmem_limit_bytes=vmem)


def _in_proj_kernel(x_ref, w_ref, b_ref, o_ref):
    o_ref[...] = jnp.dot(x_ref[...].astype(BF16), w_ref[...],
                         preferred_element_type=F32) + b_ref[...]


def _in_proj(x2, w_bf, b):
    m, d = x2.shape
    n = w_bf.shape[1]
    tm = min(ROW_TILE, m)
    return pl.pallas_call(
        _in_proj_kernel,
        grid=(m // tm,),
        in_specs=[pl.BlockSpec((tm, d), lambda i: (i, 0)),
                  pl.BlockSpec((d, n), lambda i: (0, 0)),
                  pl.BlockSpec((1, n), lambda i: (0, 0))],
        out_specs=pl.BlockSpec((tm, n), lambda i: (i, 0)),
        out_shape=jax.ShapeDtypeStruct((m, n), F32),
        compiler_params=_cp(("parallel",)),
        name="in_proj",
    )(x2, w_bf, b)


def _mix_kernel(pm_ref, pp_ref, pn_ref, aw_ref, hw_ref, hb_ref, ya_ref, x0_ref, z_ref, *, cw):
    i = pl.program_id(1)
    n = pl.num_programs(1)
    tl = pm_ref.shape[1]
    rows = lax.broadcasted_iota(jnp.int32, (tl, 1), 0)
    has_prev = i > 0
    has_next = i < n - 1

    def conv3(cur, prev_row, next_row, w):
        prev_row = jnp.where(has_prev, prev_row, 0.0)
        next_row = jnp.where(has_next, next_row, 0.0)
        dn = jnp.where(rows == 0, prev_row, pltpu.roll(cur, 1, 0))
        up = jnp.where(rows == tl - 1, next_row, pltpu.roll(cur, tl - 1, 0))
        return dn * w[0:1] + cur * w[1:2] + up * w[2:3]

    def sec(ref, r0, r1, j):
        return ref[0, r0:r1, j * cw:(j + 1) * cw]

    g = sec(pm_ref, 0, tl, 1) * sec(pm_ref, 0, tl, 2)
    gp = sec(pp_ref, 7, 8, 1) * sec(pp_ref, 7, 8, 2)
    gn = sec(pn_ref, 0, 1, 1) * sec(pn_ref, 0, 1, 2)
    ya_ref[0] = sec(pm_ref, 0, tl, 0) * conv3(g, gp, gn, aw_ref[...])

    def hconv(j):
        c = 3 + j
        w = hw_ref[:, j * cw:(j + 1) * cw]
        return conv3(sec(pm_ref, 0, tl, c), sec(pp_ref, 7, 8, c), sec(pn_ref, 0, 1, c), w) \
            + hb_ref[:, j * cw:(j + 1) * cw]

    x0_ref[0] = hconv(0)
    z_ref[0] = hconv(2) * hconv(1)


def _mix(p3, a_conv_w, h_conv_w, h_conv_b, cw):
    b, l, pw = p3.shape
    tl = min(ROW_TILE, l)
    nl = l // tl
    r8 = tl // 8
    out = jax.ShapeDtypeStruct((b, l, cw), F32)
    ospec = pl.BlockSpec((1, tl, cw), lambda bi, i: (bi, i, 0))
    return pl.pallas_call(
        functools.partial(_mix_kernel, cw=cw),
        grid=(b, nl),
        in_specs=[pl.BlockSpec((1, tl, pw), lambda bi, i: (bi, i, 0)),
                  pl.BlockSpec((1, 8, pw), lambda bi, i: (bi, jnp.maximum(i * r8 - 1, 0), 0)),
                  pl.BlockSpec((1, 8, pw), lambda bi, i: (bi, jnp.minimum((i + 1) * r8, l // 8 - 1), 0)),
                  pl.BlockSpec((SHORT_K, cw), lambda bi, i: (0, 0)),
                  pl.BlockSpec((SHORT_K, 3 * cw), lambda bi, i: (0, 0)),
                  pl.BlockSpec((1, 3 * cw), lambda bi, i: (0, 0))],
        out_specs=[ospec, ospec, ospec],
        out_shape=[out, out, out],
        compiler_params=_cp(("parallel", "parallel")),
        name="mix",
    )(p3, p3, p3, a_conv_w, h_conv_w, h_conv_b)


def _filter_kernel(t_ref, w_ref, f_ref, w1t_ref, w1c_ref, w1s_ref, b1_ref, fr_ref, w2_ref, b2_ref,
                   w3_ref, dec_ref, hk_ref, asum_ref, *, hw):
    i = pl.program_id(0)
    tl = t_ref.shape[0]
    t = t_ref[...]
    ang = w_ref[...] * f_ref[...]
    z1 = (t * w1t_ref[...]
          + jnp.dot(jnp.cos(ang), w1c_ref[...], precision=HI, preferred_element_type=F32)
          + jnp.dot(-jnp.sin(ang), w1s_ref[...], precision=HI, preferred_element_type=F32)
          + b1_ref[...])
    fr = fr_ref[...]
    h = jnp.sin(fr * z1)
    h = jnp.sin(fr * (jnp.dot(h, w2_ref[...], precision=HI, preferred_element_type=F32) + b2_ref[...]))
    h = jnp.dot(h, w3_ref[...], precision=HI, preferred_element_type=F32)
    h = h * jnp.exp(-t * jnp.abs(dec_ref[...]))
    rows = i * tl + lax.broadcasted_iota(jnp.int32, (tl, 1), 0)
    fwd = h[:, :hw]
    bwd = jnp.where(rows == 0, 0.0, h[:, hw:])
    hk_ref[0] = fwd
    hk_ref[1] = bwd
    s = jnp.concatenate([jnp.sum(jnp.abs(fwd), axis=0, keepdims=True),
                         jnp.sum(jnp.abs(bwd), axis=0, keepdims=True)], axis=0)

    @pl.when(i == 0)
    def _():
        asum_ref[...] = jnp.zeros_like(asum_ref)

    asum_ref[...] += s


def _pad2(a, r, c):
    return jnp.zeros((r, c), F32).at[:a.shape[0], :a.shape[1]].set(a.astype(F32))


def _hyena_filter(l, hf_w1, hf_b1, hf_freq, hf_w2, hf_b2, hf_w3, hy_decay):
    bands = (hf_w1.shape[0] - 1) // 2
    assert bands <= LANES and hf_w1.shape[1] <= LANES, "filter features and hidden width are padded to one lane tile"
    hw = hy_decay.shape[1]
    t = jnp.linspace(0.0, 1.0, l, dtype=F32)[:, None]
    w = (2.0 * math.pi) * jnp.arange(l, dtype=F32)[:, None] / l
    f = jnp.linspace(1e-4, bands - 1, bands, dtype=F32)[None, :]
    p = LANES
    tl = min(ROW_TILE, l)
    args = (t, w, _pad2(f, 1, p), _pad2(hf_w1[0:1], 1, p), _pad2(hf_w1[1:1 + bands], p, p),
            _pad2(hf_w1[1 + bands:], p, p), _pad2(hf_b1[None], 1, p), _pad2(hf_freq[None], 1, p),
            _pad2(hf_w2, p, p), _pad2(hf_b2[None], 1, p), _pad2(hf_w3, p, 2 * hw),
            hy_decay.reshape(1, 2 * hw).astype(F32))
    const = lambda shape: pl.BlockSpec(shape, lambda i: (0, 0))
    return pl.pallas_call(
        functools.partial(_filter_kernel, hw=hw),
        grid=(l // tl,),
        in_specs=[pl.BlockSpec((tl, 1), lambda i: (i, 0)), pl.BlockSpec((tl, 1), lambda i: (i, 0)),
                  const((1, p)), const((1, p)), const((p, p)), const((p, p)), const((1, p)),
                  const((1, p)), const((p, p)), const((1, p)), const((p, 2 * hw)), const((1, 2 * hw))],
        out_specs=[pl.BlockSpec((2, tl, hw), lambda i: (0, i, 0)),
                   pl.BlockSpec((2, hw), lambda i: (0, 0))],
        out_shape=[jax.ShapeDtypeStruct((2, l, hw), F32), jax.ShapeDtypeStruct((2, hw), F32)],
        compiler_params=_cp(("arbitrary",)),
        name="hyena_filter",
    )(*args)


def _dft_consts(l):
    n2s = DFT_N2
    n = 2 * l
    n1s = n // n2s
    two_pi = 2.0 * math.pi
    k1 = jnp.arange(n1s, dtype=jnp.int32)[:, None]
    n1 = jnp.arange(n1s // 2, dtype=jnp.int32)[None, :]
    ang = -two_pi * ((k1 * n1) % n1s).astype(F32) / n1s
    fa = jnp.concatenate([jnp.cos(ang), jnp.sin(ang)], axis=0)
    k1 = jnp.arange(n1s, dtype=jnp.int32)[:, None, None]
    k2 = jnp.arange(n2s, dtype=jnp.int32)[None, :, None]
    n2 = jnp.arange(n2s, dtype=jnp.int32)[None, None, :]
    ang = -two_pi * ((n2 * k2 * n1s + n2 * k1) % n).astype(F32) / n
    mr, mi = jnp.cos(ang), jnp.sin(ang)
    g = jnp.concatenate([jnp.concatenate([mr, -mi], 2), jnp.concatenate([mi, mr], 2)], 1)
    mrt, mit = jnp.swapaxes(mr, 1, 2), jnp.swapaxes(mi, 1, 2)
    gi = jnp.concatenate([jnp.concatenate([mrt, mit], 2), jnp.concatenate([-mit, mrt], 2)], 1)
    n1 = jnp.arange(n1s // 2, dtype=jnp.int32)[:, None]
    k1 = jnp.arange(n1s, dtype=jnp.int32)[None, :]
    ang = two_pi * ((n1 * k1) % n1s).astype(F32) / n1s
    fb = jnp.concatenate([jnp.cos(ang), -jnp.sin(ang)], axis=1) / n
    return fa, g, gi, fb


def _mm(a, b, precise):
    dot = lambda x, y: jnp.dot(x, y, preferred_element_type=F32)
    a_hi, b_hi = a.astype(BF16), b.astype(BF16)
    if not precise:
        return dot(a_hi, b_hi)
    a_lo = (a - a_hi.astype(F32)).astype(BF16)
    b_lo = (b - b_hi.astype(F32)).astype(BF16)
    return dot(a_hi, b_hi) + (dot(a_hi, b_lo) + dot(a_lo, b_hi))


def _lmat_kernel(f_ref, z_ref, o_ref, *, precise):
    o_ref[0] = _mm(f_ref[...], z_ref[0], precise).astype(o_ref.dtype)


def _dft_stage1(fa, z3, precise):
    b, r, cols = z3.shape
    m = fa.shape[0]
    tn = min(DFT_COL_TILE, cols)
    return pl.pallas_call(
        functools.partial(_lmat_kernel, precise=precise),
        grid=(b, cols // tn),
        in_specs=[pl.BlockSpec((m, r), lambda bi, j: (0, 0)),
                  pl.BlockSpec((1, r, tn), lambda bi, j: (bi, 0, j))],
        out_specs=pl.BlockSpec((1, m, tn), lambda bi, j: (bi, 0, j)),
        out_shape=jax.ShapeDtypeStruct((b, m, cols), F32 if precise else BF16),
        compiler_params=_cp(("parallel", "parallel")),
        name="dft_stage1",
    )(fa, z3)


def _kf_kernel(a_ref, g_ref, asum_ref, kf_ref):
    n2 = a_ref.shape[3]
    c = a_ref.shape[4]
    g = g_ref[0]
    f0 = _mm(g, a_ref[0, :, 0].reshape(2 * n2, c), True)
    f1 = _mm(g, a_ref[1, :, 0].reshape(2 * n2, c), True)
    inv = 1.0 / (asum_ref[0:1] + asum_ref[1:2])
    kf_ref[0] = jnp.concatenate([(f0[:n2] + f1[:n2]) * inv, (f0[n2:] - f1[n2:]) * inv], axis=0)


def _filter_spectrum(a5, g, asum):
    _, _, n1s, n2, c = a5.shape
    return pl.pallas_call(
        _kf_kernel,
        grid=(n1s,),
        in_specs=[pl.BlockSpec((2, 2, 1, n2, c), lambda k: (0, 0, k, 0, 0)),
                  pl.BlockSpec((1, 2 * n2, 2 * n2), lambda k: (k, 0, 0)),
                  pl.BlockSpec((2, c), lambda k: (0, 0))],
        out_specs=pl.BlockSpec((1, 2 * n2, c), lambda k: (k, 0, 0)),
        out_shape=jax.ShapeDtypeStruct((n1s, 2 * n2, c), F32),
        compiler_params=_cp(("parallel",)),
        name="filter_spectrum",
    )(a5, g, asum)


def _spec_kernel(a_ref, g_ref, kf_ref, gi_ref, d_ref):
    n2 = a_ref.shape[3]
    c = a_ref.shape[4]
    kf = kf_ref[0]
    kr, ki = kf[:n2], kf[n2:]
    for bi in range(a_ref.shape[0]):
        a = a_ref[bi, :, 0].reshape(2 * n2, c)
        xf = _mm(g_ref[0], a, False)
        xr, xi = xf[:n2], xf[n2:]
        y = jnp.concatenate([xr * kr - xi * ki, xr * ki + xi * kr], axis=0)
        d = _mm(gi_ref[0], y, False)
        d_ref[bi, :, 0] = d.reshape(2, n2, c).astype(d_ref.dtype)


def _spectral_multiply(a5, g, kf, gi):
    b, _, n1s, n2, c = a5.shape
    bb = math.gcd(b, SPECTRAL_BATCH)
    blk = pl.BlockSpec((bb, 2, 1, n2, c), lambda k, bi: (bi, 0, k, 0, 0))
    mat = pl.BlockSpec((1, 2 * n2, 2 * n2), lambda k, bi: (k, 0, 0))
    return pl.pallas_call(
        _spec_kernel,
        grid=(n1s, b // bb),
        in_specs=[blk, mat, pl.BlockSpec((1, 2 * n2, c), lambda k, bi: (k, 0, 0)), mat],
        out_specs=blk,
        out_shape=jax.ShapeDtypeStruct(a5.shape, a5.dtype),
        compiler_params=_cp(("parallel", "arbitrary")),
        name="spectral_multiply",
    )(a5, g, kf, gi)


def _inv_kernel(f_ref, d_ref, z_ref, x0_ref, bias_ref, o_ref):
    y = _mm(f_ref[...], d_ref[0], False)
    o_ref[0] = x0_ref[0] * (y + z_ref[0] * bias_ref[...])


def _dft_final(fb, d3, z3, x03, bias_t):
    b, m2, cols = d3.shape
    r = fb.shape[0]
    tn = min(DFT_COL_TILE, cols)
    rspec = pl.BlockSpec((1, r, tn), lambda bi, j: (bi, 0, j))
    return pl.pallas_call(
        _inv_kernel,
        grid=(b, cols // tn),
        in_specs=[pl.BlockSpec((r, m2), lambda bi, j: (0, 0)),
                  pl.BlockSpec((1, m2, tn), lambda bi, j: (bi, 0, j)),
                  rspec, rspec,
                  pl.BlockSpec((1, tn), lambda bi, j: (0, j))],
        out_specs=rspec,
        out_shape=jax.ShapeDtypeStruct((b, r, cols), F32),
        compiler_params=_cp(("parallel", "parallel")),
        name="dft_final",
    )(fb, d3, z3, x03, bias_t)


def _hyena_mix(z, x0, hy_bias, hf):
    b, l, c = z.shape
    n2 = DFT_N2
    n1s = 2 * l // n2
    fa, g, gi, fb = _dft_consts(l)
    hk, asum = _hyena_filter(l, *hf)
    cols = n2 * c
    ka = _dft_stage1(fa, hk.reshape(2, n1s // 2, cols), True)
    kf = _filter_spectrum(ka.reshape(2, 2, n1s, n2, c), g, asum)
    z3 = z.reshape(b, n1s // 2, cols)
    a = _dft_stage1(fa.astype(BF16), z3, False)
    d = _spectral_multiply(a.reshape(b, 2, n1s, n2, c), g.astype(BF16), kf, gi.astype(BF16))
    bias_t = jnp.tile(hy_bias.astype(F32), n2)[None, :]
    yh = _dft_final(fb.astype(BF16), d.reshape(b, 2 * n1s, cols), z3, x0.reshape(b, n1s // 2, cols), bias_t)
    return yh.reshape(b, l, c)


def _layer_norm(r, g, b):
    mu = jnp.mean(r, axis=-1, keepdims=True)
    rc = r - mu
    var = jnp.mean(rc * rc, axis=-1, keepdims=True)
    return rc * lax.rsqrt(var + LN_EPS) * g + b


def _out_proj_kernel(ya_ref, yh_ref, x_ref, wa_ref, wh_ref, g_ref, b_ref, o_ref, *, alpha):
    mix = (jnp.dot(ya_ref[...].astype(BF16), wa_ref[...], preferred_element_type=F32)
           + jnp.dot(yh_ref[...].astype(BF16), wh_ref[...], preferred_element_type=F32))
    o_ref[...] = _layer_norm(alpha * x_ref[...] + mix, g_ref[...], b_ref[...])


def _out_proj(ya, yh, x2, wa_bf, wh_bf, g, b, alpha):
    m, d = x2.shape
    cw = ya.shape[1]
    tm = min(ROW_TILE, m)
    row = lambda w: pl.BlockSpec((tm, w), lambda i: (i, 0))
    const = lambda shape: pl.BlockSpec(shape, lambda i: (0, 0))
    return pl.pallas_call(
        functools.partial(_out_proj_kernel, alpha=alpha),
        grid=(m // tm,),
        in_specs=[row(cw), row(cw), row(d), const((cw, d)), const((cw, d)), const((1, d)), const((1, d))],
        out_specs=row(d),
        out_shape=jax.ShapeDtypeStruct((m, d), F32),
        compiler_params=_cp(("parallel",)),
        name="out_proj_ln",
    )(ya, yh, x2, wa_bf, wh_bf, g, b)


NEG_INF = float("-inf")
NO_ROW = float(2 ** 30)


def _topk_keys(s, k):
    n, t = s.shape
    sub = lax.broadcasted_iota(jnp.int32, (SUBLANES, t), 0).astype(F32)
    slabs = [s[g * SUBLANES:(g + 1) * SUBLANES] for g in range(n // SUBLANES)]
    ids = [sub + float(g * SUBLANES) for g in range(n // SUBLANES)]
    vals, poss = [], []
    for _ in range(k):
        vs, rs = slabs, ids
        while len(vs) > 1:
            nv, nr = [], []
            for a in range(0, len(vs), 2):
                keep = vs[a] >= vs[a + 1]
                nv.append(jnp.maximum(vs[a], vs[a + 1]))
                nr.append(jnp.where(keep, rs[a], rs[a + 1]))
            vs, rs = nv, nr
        m = jnp.max(vs[0], axis=0, keepdims=True)
        pos = jnp.min(jnp.where(vs[0] == m, rs[0], NO_ROW), axis=0, keepdims=True)
        vals.append(m)
        poss.append(pos)
        slabs = [jnp.where(r == pos, NEG_INF, v) for v, r in zip(slabs, ids)]
    return jnp.concatenate(vals, axis=0), jnp.concatenate(poss, axis=0)


def _topk_ordered(s, order, payload, k):
    vals, pays = [], []
    for _ in range(k):
        m = jnp.max(s, axis=0, keepdims=True)
        first = jnp.min(jnp.where(s == m, order, NO_ROW), axis=0, keepdims=True)
        hit = order == first
        vals.append(m)
        pays.append(jnp.max(jnp.where(hit, payload, -1.0), axis=0, keepdims=True))
        s = jnp.where(hit, NEG_INF, s)
    return jnp.concatenate(vals, axis=0), jnp.concatenate(pays, axis=0)


def _pair_candidates(v1, i1, v2, i2, nkeys):
    kk, t = v1.shape
    up8 = lambda x: -(-x // SUBLANES) * SUBLANES
    lim = lambda a: kk // (a + 1)
    nfull = sum(1 for a in range(kk) if lim(a) >= SUBLANES)
    blocks = [("a", a, 0, lim(a)) for a in range(nfull)]
    blocks += [("b", b, nfull, lim(b)) for b in range(nfull)]
    blocks += [("a", a, nfull, lim(a)) for a in range(nfull, kk) if lim(a) > nfull]
    assert sum(hi - lo for _, _, lo, hi in blocks) == sum(lim(a) for a in range(kk))
    sums, poss, eids = [], [], []
    for kind, fixed, lo, hi in blocks:
        rows = up8(hi)
        it = lax.broadcasted_iota(jnp.int32, (rows, t), 0).astype(F32)
        valid = (it >= float(lo)) & (it < float(hi))
        if kind == "a":
            sm = v1[fixed:fixed + 1] + v2[:rows]
            ps = it + float(fixed * kk)
            ei = i1[fixed:fixed + 1] * float(nkeys) + i2[:rows]
        else:
            sm = v1[:rows] + v2[fixed:fixed + 1]
            ps = it * float(kk) + float(fixed)
            ei = i1[:rows] * float(nkeys) + i2[fixed:fixed + 1]
        sums.append(jnp.where(valid, sm, NEG_INF))
        poss.append(ps)
        eids.append(ei)
    cat = lambda xs: jnp.concatenate(xs, axis=0)
    return cat(sums), cat(poss), cat(eids)


def _route_kernel(x_ref, wq_ref, keys_ref, eidx_ref, gate_ref, q_scr, e_scr, g_scr, *, heads, nkeys, row_scale):
    kk = PEER_TOPK
    half = keys_ref.shape[2]
    q_scr[...] = jnp.dot(x_ref[...].astype(BF16), wq_ref[...], preferred_element_type=F32)

    def sub_key_tops(h):
        tops = []
        for p in range(2):
            c = h * 2 + p
            qc = q_scr[:, pl.ds(pl.multiple_of(c * half, half), half)].astype(BF16)
            st = lax.dot_general(keys_ref[c], qc, (((1,), (1,)), ((), ())),
                                 preferred_element_type=F32)
            tops.extend(_topk_keys(st, kk))
        return tuple(tops)

    def pick_experts(h, tops):
        v1, i1, v2, i2 = tops
        cand, cpos, cidx = _pair_candidates(v1, i1, v2, i2, nkeys)
        best, e = _topk_ordered(cand, cpos, cidx, kk)
        ex = jnp.exp(best - jnp.max(best, axis=0, keepdims=True))
        gate = ex / jnp.sum(ex, axis=0, keepdims=True)
        r0 = pl.multiple_of(h * kk, kk)
        e_scr[pl.ds(r0, kk), :] = e
        g_scr[pl.ds(r0, kk), :] = gate

    def step(h, tops):
        nxt = sub_key_tops(h + 1)
        pick_experts(h, tops)
        return nxt

    last = lax.fori_loop(0, heads - 1, step, sub_key_tops(0))
    pick_experts(heads - 1, last)
    eidx_ref[...] = (e_scr[...].T * float(row_scale)).astype(jnp.int32)
    gate_ref[...] = g_scr[...].T


def _route(x2, wq_bf, keys_bf, heads, nkeys):
    m, d = x2.shape
    qd = wq_bf.shape[1]
    tt = min(ROUTE_TOKENS, m)
    hk = heads * PEER_TOPK
    return pl.pallas_call(
        functools.partial(_route_kernel, heads=heads, nkeys=nkeys, row_scale=PACKED_ROWS),
        grid=(m // tt,),
        in_specs=[pl.BlockSpec((tt, d), lambda i: (i, 0)),
                  pl.BlockSpec((d, qd), lambda i: (0, 0)),
                  pl.BlockSpec(keys_bf.shape, lambda i: (0, 0, 0))],
        out_specs=[pl.BlockSpec((tt, hk), lambda i: (i, 0)),
                   pl.BlockSpec((tt, hk), lambda i: (i, 0))],
        out_shape=[jax.ShapeDtypeStruct((m, hk), jnp.int32), jax.ShapeDtypeStruct((m, hk), F32)],
        scratch_shapes=[pltpu.VMEM((tt, qd), F32), pltpu.VMEM((hk, tt), F32), pltpu.VMEM((hk, tt), F32)],
        compiler_params=_cp(("parallel",)),
        name="peer_route",
    )(x2, wq_bf, keys_bf)


def _pack_table(tab):
    e, d = tab.shape
    rows = d // (2 * LANES)
    eb = math.gcd(e, PACK_EXPERTS)
    return pl.pallas_call(
        _pack_kernel,
        grid=(e // eb,),
        in_specs=[pl.BlockSpec((eb, d), lambda i: (i, 0))],
        out_specs=pl.BlockSpec((eb * rows, LANES), lambda i: (i, 0)),
        out_shape=jax.ShapeDtypeStruct((e * rows, LANES), jnp.int32),
        compiler_params=_cp(("parallel",)),
        name="pack_table",
    )(tab.astype(F32))


def _pack_kernel(t_ref, o_ref):
    eb, d = t_ref.shape
    rows = d // (2 * LANES)
    bf16_bits = lambda x: lax.bitcast_convert_type(x.astype(BF16).astype(F32), jnp.int32)
    for j in range(rows):
        lo = bf16_bits(t_ref[:, 2 * j * LANES:(2 * j + 1) * LANES])
        hi = bf16_bits(t_ref[:, (2 * j + 1) * LANES:(2 * j + 2) * LANES])
        word = (hi & jnp.int32(-65536)) | ((lo >> 16) & jnp.int32(0xFFFF))
        o_ref[pl.ds(j, eb, stride=rows), :] = word


def _gather_rows(tab_ref, off_ref, t, nk):
    tiles = [tab_ref[pl.ds(pl.multiple_of(off_ref[t, k], PACKED_ROWS), PACKED_ROWS), :] for k in range(nk)]
    return pltpu.bitcast(jnp.concatenate(tiles, axis=0), BF16)


def _diag_mask(n):
    lane = lax.broadcasted_iota(jnp.int32, (SUBLANES, n), 1)
    return (lane & (SUBLANES - 1)) == lax.broadcasted_iota(jnp.int32, (SUBLANES, n), 0)


def _with_offsets(off_hbm, bufs, sems, body):
    i = pl.program_id(0)
    n = pl.num_programs(0)
    tt = bufs[0].shape[0]

    def fetch(step, slot):
        return pltpu.make_async_copy(off_hbm.at[pl.ds(step * tt, tt)], bufs[slot], sems.at[slot])

    @pl.when(i == 0)
    def _():
        fetch(0, 0).start()

    for slot in range(2):
        @pl.when(i % 2 == slot)
        def _(slot=slot):
            @pl.when(i + 1 < n)
            def _():
                fetch(i + 1, 1 - slot).start()

            fetch(i, slot).wait()
            body(bufs[slot])


def _peer_u_kernel(off_hbm, x_ref, u_ref, o_ref, off_a, off_b, sems):
    tt, nk = off_a.shape
    diag = _diag_mask(SUBLANES * nk)

    def body(off_ref):
        for t in range(tt):
            r = _gather_rows(u_ref, off_ref, t, nk)
            xt = x_ref[pl.ds(t, 1), :].reshape(SUBLANES, LANES).astype(BF16)
            p = lax.dot_general(xt, r, (((1,), (1,)), ((), ())), preferred_element_type=F32)
            o_ref[pl.ds(t, 1), :] = jnp.sum(jnp.where(diag, p, 0.0), axis=0, keepdims=True)

    _with_offsets(off_hbm, (off_a, off_b), sems, body)


def _peer_u(eidx, x2, u_pk):
    m, nk = eidx.shape
    ln = u_pk.shape[1]
    tt = min(EXPERT_TOKENS, m)
    n = SUBLANES * nk
    return pl.pallas_call(
        _peer_u_kernel,
        grid=(m // tt,),
        in_specs=[pl.BlockSpec(memory_space=pl.ANY),
                  pl.BlockSpec((tt, SUBLANES * ln), lambda i: (i, 0)),
                  pl.BlockSpec(u_pk.shape, lambda i: (0, 0), pipeline_mode=pl.Buffered(1))],
        out_specs=pl.BlockSpec((tt, n), lambda i: (i, 0)),
        out_shape=jax.ShapeDtypeStruct((m, n), F32),
        scratch_shapes=[pltpu.SMEM((tt, nk), jnp.int32), pltpu.SMEM((tt, nk), jnp.int32),
                        pltpu.SemaphoreType.DMA((2,))],
        compiler_params=_cp(("arbitrary",)),
        name="peer_u",
    )(eidx, x2, u_pk)


def _dot_exact01(x, w):
    acc = None
    rest = x
    for _ in range(3):
        piece = rest.astype(BF16)
        rest = rest - piece.astype(F32)
        d = jnp.dot(piece, w, preferred_element_type=F32)
        acc = d if acc is None else acc + d
    return acc


def _coef_kernel(part_ref, gate_ref, o_ref):
    n = part_ref.shape[1]
    nk = gate_ref.shape[1]
    sub = n // nk
    fold = (lax.broadcasted_iota(jnp.int32, (n, nk), 0) // sub
            == lax.broadcasted_iota(jnp.int32, (n, nk), 1)).astype(BF16)
    spread = (lax.broadcasted_iota(jnp.int32, (nk, n), 1) // sub
              == lax.broadcasted_iota(jnp.int32, (nk, n), 0)).astype(BF16)
    a = _dot_exact01(part_ref[...], fold)
    gelu = 0.5 * a * (1.0 + lax.erf(a * (1.0 / math.sqrt(2.0))))
    o_ref[...] = _dot_exact01(gate_ref[...] * gelu, spread)


def _coef(part, gate):
    m, n = part.shape
    nk = gate.shape[1]
    tt = min(LN_ROW_TILE, m)
    return pl.pallas_call(
        _coef_kernel,
        grid=(m // tt,),
        in_specs=[pl.BlockSpec((tt, n), lambda i: (i, 0)),
                  pl.BlockSpec((tt, nk), lambda i: (i, 0))],
        out_specs=pl.BlockSpec((tt, n), lambda i: (i, 0)),
        out_shape=jax.ShapeDtypeStruct((m, n), F32),
        compiler_params=_cp(("parallel",)),
        name="peer_coef",
    )(part, gate)


def _peer_v_kernel(off_hbm, c_ref, v_ref, o_ref, off_a, off_b, sems):
    tt, nk = off_a.shape
    diag = _diag_mask(SUBLANES * nk)

    def body(off_ref):
        for t in range(tt):
            r = _gather_rows(v_ref, off_ref, t, nk)
            c8 = jnp.where(diag, c_ref[pl.ds(t, 1), :], 0.0).astype(BF16)
            y = jnp.dot(c8, r, preferred_element_type=F32)
            o_ref[pl.ds(t, 1), :] = y.reshape(1, SUBLANES * LANES)

    _with_offsets(off_hbm, (off_a, off_b), sems, body)


def _peer_v(eidx, coef8, v_pk):
    m, nk = eidx.shape
    ln = v_pk.shape[1]
    tt = min(EXPERT_TOKENS, m)
    n = SUBLANES * nk
    return pl.pallas_call(
        _peer_v_kernel,
        grid=(m // tt,),
        in_specs=[pl.BlockSpec(memory_space=pl.ANY),
                  pl.BlockSpec((tt, n), lambda i: (i, 0)),
                  pl.BlockSpec(v_pk.shape, lambda i: (0, 0), pipeline_mode=pl.Buffered(1))],
        out_specs=pl.BlockSpec((tt, SUBLANES * ln), lambda i: (i, 0)),
        out_shape=jax.ShapeDtypeStruct((m, SUBLANES * ln), F32),
        scratch_shapes=[pltpu.SMEM((tt, nk), jnp.int32), pltpu.SMEM((tt, nk), jnp.int32),
                        pltpu.SemaphoreType.DMA((2,))],
        compiler_params=_cp(("arbitrary",)),
        name="peer_v",
    )(eidx, coef8, v_pk)


def _res_ln_kernel(x_ref, y_ref, g_ref, b_ref, o_ref, *, alpha):
    o_ref[...] = _layer_norm(alpha * x_ref[...] + y_ref[...], g_ref[...], b_ref[...])


def _res_ln(x2, y2, g, b, alpha):
    m, d = x2.shape
    tm = min(LN_ROW_TILE, m)
    row = pl.BlockSpec((tm, d), lambda i: (i, 0))
    const = pl.BlockSpec((1, d), lambda i: (0, 0))
    return pl.pallas_call(
        functools.partial(_res_ln_kernel, alpha=alpha),
        grid=(m // tm,),
        in_specs=[row, row, const, const],
        out_specs=row,
        out_shape=jax.ShapeDtypeStruct((m, d), F32),
        compiler_params=_cp(("parallel",)),
        name="residual_ln",
    )(x2, y2, g, b)


def _peer(x2, wq_bf, keys_bf, u_pk, v_pk, heads, nkeys):
    m, d = x2.shape
    assert d == SUBLANES * LANES, "the expert stages keep one token row per (8, 128) vreg"
    eidx, gate = _route(x2, wq_bf, keys_bf, heads, nkeys)
    part = _peer_u(eidx, x2, u_pk)
    coef8 = _coef(part, gate)
    return _peer_v(eidx, coef8, v_pk)


def _encoder_layer(x, lw, alpha):
    b, l, d = x.shape
    cw = lw["a_conv_w"].shape[1]
    x2 = x.reshape(b * l, d)
    p = _in_proj(x2, lw["w_in"], lw["b_in"])
    ya, x0, z = _mix(p.reshape(b, l, -1), lw["a_conv_w"], lw["h_conv_w"], lw["h_conv_b"], cw)
    yh = _hyena_mix(z, x0, lw["hy_bias"], lw["hf"])
    x1 = _out_proj(ya.reshape(b * l, cw), yh.reshape(b * l, -1), x2, lw["w_out_a"], lw["w_out_h"],
                   lw["ln1_g"], lw["ln1_b"], alpha)
    y = _peer(x1, lw["peer_wq"], lw["peer_keys"], lw["peer_u"], lw["peer_v"], lw["heads"], lw["nkeys"])
    return _res_ln(x1, y, lw["ln2_g"], lw["ln2_b"], alpha).reshape(b, l, d)


def kernel(x_prompt, x_sample, w_in, b_in, a_conv_w, h_conv_w, h_conv_b, hf_w1, hf_b1, hf_freq, hf_w2,
           hf_b2, hf_w3, hy_decay, hy_bias, w_out, ln1_g, ln1_b, peer_wq, peer_keys, peer_u, peer_v,
           ln2_g, ln2_b):
    depth = w_in.shape[0]
    alpha = (2.0 * depth) ** 0.25
    layers = []
    for i in range(depth):
        cw = a_conv_w.shape[2]
        heads, _, nkeys, half = peer_keys.shape[1:]
        layers.append(dict(
            w_in=w_in[i].astype(BF16), b_in=b_in[i][None].astype(F32),
            a_conv_w=a_conv_w[i], h_conv_w=h_conv_w[i], h_conv_b=h_conv_b[i][None],
            hf=(hf_w1[i], hf_b1[i], hf_freq[i], hf_w2[i], hf_b2[i], hf_w3[i], hy_decay[i]),
            hy_bias=hy_bias[i],
            w_out_a=w_out[i, :cw].astype(BF16), w_out_h=w_out[i, cw:].astype(BF16),
            ln1_g=ln1_g[i][None], ln1_b=ln1_b[i][None],
            peer_wq=peer_wq[i].astype(BF16),
            peer_keys=peer_keys[i].reshape(heads * 2, nkeys, half).astype(BF16),
            peer_u=_pack_table(peer_u[i]), peer_v=_pack_table(peer_v[i]),
            ln2_g=ln2_g[i][None], ln2_b=ln2_b[i][None], heads=heads, nkeys=nkeys))

    def trunk(x):
        for lw in layers:
            x = _encoder_layer(x, lw, alpha)
        return x

    return trunk(x_prompt), trunk(x_sample)
```
